```python
import math
import jax, jax.numpy as jnp
from jax import lax
import numpy as np

D_MODEL = 2048
BATCH = 1
SEQ = 8192
DEPTH = 4
DEC_BATCH = 1
DEC_SEQ = 16384
PAST_LEN = 128

N_MIXERS = 2
N_SSM_LAYERS = (DEPTH + 1) // 2
N_ATTN_LAYERS = DEPTH // 2
PLE_DIM = 256
GRID_W = 64
SSM_WIDTH = D_MODEL
SSM_GROUP = 16
SSM_GROUPS = SSM_WIDTH // SSM_GROUP
SSM_STATE = 64
SCAN_CHUNK = 128
DT_MIN = 0.001
DT_MAX = 0.1
HEAD_DIM = 128
N_HEADS = D_MODEL // HEAD_DIM
WIN_H = 8
WIN_W = 16
D_FF = -(-8 * D_MODEL // (3 * 256)) * 256
EPS = 1e-6

kernel_name = "hybrid_s5_natten_encoder"


def _rmsnorm(x, g):
    xf = x.astype(jnp.float32)
    y = xf * lax.rsqrt(jnp.mean(xf * xf, axis=-1, keepdims=True) + EPS)
    return (y * g.astype(jnp.float32)).astype(x.dtype)


def _complex_affine_combine(e1, e2):
    a1r, a1i, b1r, b1i = e1
    a2r, a2i, b2r, b2i = e2
    ar = a2r * a1r - a2i * a1i
    ai = a2r * a1i + a2i * a1r
    br = a2r * b1r - a2i * b1i + b2r
    bi = a2r * b1i + a2i * b1r + b2i
    return (ar, ai, br, bi)


def _s5_scan(u, a_re, a_im, log_dt, b_re, b_im, c_re, c_im):
    f32 = jnp.float32
    a_re, a_im = a_re.astype(f32), a_im.astype(f32)
    b_re, b_im = b_re.astype(f32), b_im.astype(f32)
    c_re, c_im = c_re.astype(f32), c_im.astype(f32)
    dt = jnp.exp(log_dt.astype(f32))[:, None]
    mag = jnp.exp(a_re * dt)
    lam_re = mag * jnp.cos(a_im * dt)
    lam_im = mag * jnp.sin(a_im * dt)
    den = a_re * a_re + a_im * a_im
    xr, xi = lam_re - 1.0, lam_im
    f_re = (xr * a_re + xi * a_im) / den
    f_im = (xi * a_re - xr * a_im) / den
    bb_re = f_re[..., None] * b_re - f_im[..., None] * b_im
    bb_im = f_re[..., None] * b_im + f_im[..., None] * b_re

    bsz, seq = u.shape[0], u.shape[1]
    n_chunks = seq // SCAN_CHUNK
    uc = jnp.moveaxis(u.reshape(bsz, n_chunks, SCAN_CHUNK, SSM_GROUPS, SSM_GROUP), 1, 0)
    shp = (bsz, SCAN_CHUNK, SSM_GROUPS, SSM_STATE)
    lam_seq_re = jnp.broadcast_to(lam_re, shp)
    lam_seq_im = jnp.broadcast_to(lam_im, shp)

    def step(carry, u_blk):
        h_re, h_im = carry
        bu_re = jnp.einsum('btgc,gpc->btgp', u_blk, bb_re)
        bu_im = jnp.einsum('btgc,gpc->btgp', u_blk, bb_im)
        pw_re, pw_im, x_re, x_im = lax.associative_scan(
            _complex_affine_combine, (lam_seq_re, lam_seq_im, bu_re, bu_im), axis=1)
        s_re = x_re + pw_re * h_re[:, None] - pw_im * h_im[:, None]
        s_im = x_im + pw_re * h_im[:, None] + pw_im * h_re[:, None]
        y = (jnp.einsum('btgp,gcp->btgc', s_re, c_re)
             - jnp.einsum('btgp,gcp->btgc', s_im, c_im))
        return (s_re[:, -1], s_im[:, -1]), y

    init = (jnp.zeros((bsz, SSM_GROUPS, SSM_STATE), f32),
            jnp.zeros((bsz, SSM_GROUPS, SSM_STATE), f32))
    _, ys = lax.scan(step, init, uc)
    return jnp.moveaxis(ys, 0, 1).reshape(bsz, seq, SSM_GROUPS, SSM_GROUP)


def _s5_mixer(x, w_in, a_re, a_im, log_dt, b_re, b_im, c_re, c_im, d, w_glu):
    bsz, seq, _ = x.shape
    u = (x @ w_in).astype(jnp.float32)
    ug = u.reshape(bsz, seq, SSM_GROUPS, SSM_GROUP)
    y_f = _s5_scan(ug, a_re[0], a_im[0], log_dt[0], b_re[0], b_im[0], c_re[0], c_im[0])
    y_b = jnp.flip(_s5_scan(jnp.flip(ug, axis=1), a_re[1], a_im[1], log_dt[1],
                            b_re[1], b_im[1], c_re[1], c_im[1]), axis=1)
    y = (y_f + y_b).reshape(bsz, seq, SSM_WIDTH) + d.astype(jnp.float32) * u
    g = jax.nn.gelu(y).astype(x.dtype)
    ab = g @ w_glu
    return ab[..., :D_MODEL] * jax.nn.sigmoid(ab[..., D_MODEL:])


def _head_rmsnorm(t, g):
    tf = t.astype(jnp.float32)
    y = tf * lax.rsqrt(jnp.mean(tf * tf, axis=-1, keepdims=True) + EPS)
    return (y * g.astype(jnp.float32)).astype(t.dtype)


def _neighborhood_attention(x, w_qkv, q_gain, k_gain, rpb, w_o):
    bsz, seq, _ = x.shape
    rows = seq // GRID_W
    kh = min(WIN_H, rows)
    qkv = (x @ w_qkv).reshape(bsz, seq, 3, N_HEADS, HEAD_DIM)
    q = _head_rmsnorm(qkv[:, :, 0], q_gain) * (HEAD_DIM ** -0.5)
    k = _head_rmsnorm(qkv[:, :, 1], k_gain)
    v = qkv[:, :, 2]
    q_grid = q.reshape(bsz, rows, GRID_W, N_HEADS, HEAD_DIM)
    k_grid = k.reshape(bsz, rows, GRID_W, N_HEADS, HEAD_DIM)
    v_grid = v.reshape(bsz, rows, GRID_W, N_HEADS, HEAD_DIM)

    cols = jnp.arange(GRID_W)
    col_start = jnp.clip(cols - WIN_W // 2, 0, GRID_W - WIN_W)
    col_idx = col_start[:, None] + jnp.arange(WIN_W)[None, :]
    col_off = col_idx - cols[:, None] + (WIN_W - 1)
    rpb_cols = rpb[:, :, col_off]

    def row_block(r):
        rs = jnp.clip(r - kh // 2, 0, rows - kh)
        q_r = lax.dynamic_index_in_dim(q_grid, r, axis=1, keepdims=False)
        k_rows = lax.dynamic_slice_in_dim(k_grid, rs, kh, axis=1)
        v_rows = lax.dynamic_slice_in_dim(v_grid, rs, kh, axis=1)
        k_win = k_rows[:, :, col_idx]
        v_win = v_rows[:, :, col_idx]
        row_off = rs + jnp.arange(kh) - r + (WIN_H - 1)
        bias = jnp.take(rpb_cols, row_off, axis=1)
        bias = jnp.transpose(bias, (0, 2, 1, 3)).astype(jnp.float32)
        s = jnp.einsum('bqhd,brqkhd->bhqrk', q_r, k_win).astype(jnp.float32) + bias[None]
        p = jax.nn.softmax(s.reshape(bsz, N_HEADS, GRID_W, kh * WIN_W), axis=-1)
        p = p.reshape(bsz, N_HEADS, GRID_W, kh, WIN_W).astype(v.dtype)
        return jnp.einsum('bhqrk,brqkhd->bqhd', p, v_win)

    out = lax.map(row_block, jnp.arange(rows))
    out = jnp.moveaxis(out, 0, 1).reshape(bsz, seq, N_HEADS * HEAD_DIM)
    return out @ w_o


def _swiglu(x, w_gate, w_up, w_down):
    return (jax.nn.silu(x @ w_gate) * (x @ w_up)) @ w_down


def _trunk(x, p, norm_mix, norm_ffn, norm_ple, s5_w_in, s5_a_re, s5_a_im, s5_log_dt,
           s5_b_re, s5_b_im, s5_c_re, s5_c_im, s5_d, s5_w_glu, attn_w_qkv, attn_q_norm,
           attn_k_norm, attn_rpb, attn_w_o, ffn_w_gate, ffn_w_up, ffn_w_down,
           ple_w_gate, ple_w_proj):
    h = x
    for i in range(DEPTH):
        j = i // N_MIXERS
        hn = _rmsnorm(h, norm_mix[i])
        if i % N_MIXERS == 0:
            mix = _s5_mixer(hn, s5_w_in[j], s5_a_re[j], s5_a_im[j], s5_log_dt[j],
                            s5_b_re[j], s5_b_im[j], s5_c_re[j], s5_c_im[j], s5_d[j], s5_w_glu[j])
        else:
            mix = _neighborhood_attention(hn, attn_w_qkv[j], attn_q_norm[j], attn_k_norm[j],
                                          attn_rpb[j], attn_w_o[j])
        h = h + mix.astype(h.dtype)
        h = h + _swiglu(_rmsnorm(h, norm_ffn[i]), ffn_w_gate[i], ffn_w_up[i], ffn_w_down[i])
        gate = jax.nn.sigmoid(_rmsnorm(h, norm_ple[i]) @ ple_w_gate[i])
        h = h + gate * (p[i] @ ple_w_proj[i])
    return h


def setup_inputs(seed: int = 0) -> dict:
    key = jax.random.key(seed)
    ks = jax.random.split(key, 32)
    f32 = jnp.float32

    def nrm(k, shape, scale):
        return jax.random.normal(k, shape, f32) * scale

    ns, na = N_SSM_LAYERS, N_ATTN_LAYERS
    G, P, GC = SSM_GROUPS, SSM_STATE, SSM_GROUP
    a_im_base = math.pi * jnp.arange(P, dtype=f32)
    return {
        "x_prompt": nrm(ks[0], (BATCH, SEQ, D_MODEL), 1.0),
        "x_sample": nrm(ks[1], (DEC_BATCH, DEC_SEQ, D_MODEL), 1.0),
        "p_prompt": nrm(ks[2], (DEPTH, BATCH, SEQ, PLE_DIM), 1.0),
        "p_sample": nrm(ks[3], (DEPTH, DEC_BATCH, DEC_SEQ, PLE_DIM), 1.0),
        "norm_mix": 1.0 + nrm(ks[4], (DEPTH, D_MODEL), 0.02),
        "norm_ffn": 1.0 + nrm(ks[5], (DEPTH, D_MODEL), 0.02),
        "norm_ple": 1.0 + nrm(ks[6], (DEPTH, D_MODEL), 0.02),
        "s5_w_in": nrm(ks[7], (ns, D_MODEL, SSM_WIDTH), D_MODEL ** -0.5),
        "s5_a_re": -0.5 * jnp.exp(nrm(ks[8], (ns, 2, G, P), 0.05)),
        "s5_a_im": a_im_base + nrm(ks[9], (ns, 2, G, P), 0.01),
        "s5_log_dt": jax.random.uniform(ks[10], (ns, 2, G), f32,
                                        math.log(DT_MIN), math.log(DT_MAX)),
        "s5_b_re": nrm(ks[11], (ns, 2, G, P, GC), (2 * GC) ** -0.5),
        "s5_b_im": nrm(ks[12], (ns, 2, G, P, GC), (2 * GC) ** -0.5),
        "s5_c_re": nrm(ks[13], (ns, 2, G, GC, P), (2 * P) ** -0.5),
        "s5_c_im": nrm(ks[14], (ns, 2, G, GC, P), (2 * P) ** -0.5),
        "s5_d": 1.0 + nrm(ks[15], (ns, SSM_WIDTH), 0.1),
        "s5_w_glu": nrm(ks[16], (ns, SSM_WIDTH, 2 * D_MODEL), SSM_WIDTH ** -0.5),
        "attn_w_qkv": nrm(ks[17], (na, D_MODEL, 3 * N_HEADS * HEAD_DIM), D_MODEL ** -0.5),
        "attn_q_norm": 1.0 + nrm(ks[18], (na, HEAD_DIM), 0.02),
        "attn_k_norm": 1.0 + nrm(ks[19], (na, HEAD_DIM), 0.02),
        "attn_rpb": nrm(ks[20], (na, N_HEADS, 2 * WIN_H - 1, 2 * WIN_W - 1), 0.1),
        "attn_w_o": nrm(ks[21], (na, N_HEADS * HEAD_DIM, D_MODEL), D_MODEL ** -0.5),
        "ffn_w_gate": nrm(ks[22], (DEPTH, D_MODEL, D_FF), D_MODEL ** -0.5),
        "ffn_w_up": nrm(ks[23], (DEPTH, D_MODEL, D_FF), D_MODEL ** -0.5),
        "ffn_w_down": nrm(ks[24], (DEPTH, D_FF, D_MODEL), D_FF ** -0.5),
        "ple_w_gate": nrm(ks[25], (DEPTH, D_MODEL, D_MODEL), D_MODEL ** -0.5),
        "ple_w_proj": nrm(ks[26], (DEPTH, PLE_DIM, D_MODEL), PLE_DIM ** -0.5),
    }


def reference(x_prompt, x_sample, p_prompt, p_sample, norm_mix, norm_ffn, norm_ple,
              s5_w_in, s5_a_re, s5_a_im, s5_log_dt, s5_b_re, s5_b_im, s5_c_re, s5_c_im,
              s5_d, s5_w_glu, attn_w_qkv, attn_q_norm, attn_k_norm, attn_rpb, attn_w_o,
              ffn_w_gate, ffn_w_up, ffn_w_down, ple_w_gate, ple_w_proj):
    y_prompt = _trunk(x_prompt, p_prompt, norm_mix, norm_ffn, norm_ple, s5_w_in, s5_a_re,
                      s5_a_im, s5_log_dt, s5_b_re, s5_b_im, s5_c_re, s5_c_im, s5_d, s5_w_glu,
                      attn_w_qkv, attn_q_norm, attn_k_norm, attn_rpb, attn_w_o,
                      ffn_w_gate, ffn_w_up, ffn_w_down, ple_w_gate, ple_w_proj)
    y_sample = _trunk(x_sample, p_sample, norm_mix, norm_ffn, norm_ple, s5_w_in, s5_a_re,
                      s5_a_im, s5_log_dt, s5_b_re, s5_b_im, s5_c_re, s5_c_im, s5_d, s5_w_glu,
                      attn_w_qkv, attn_q_norm, attn_k_norm, attn_rpb, attn_w_o,
                      ffn_w_gate, ffn_w_up, ffn_w_down, ple_w_gate, ple_w_proj)
    return (y_prompt, y_sample)
```

```python
import functools
import math

import numpy as np
import jax
import jax.numpy as jnp
from jax import lax
from jax.experimental import pallas as pl
from jax.experimental.pallas import tpu as pltpu

F32 = jnp.float32
BF16 = jnp.bfloat16

D_MODEL = 2048
DEPTH = 4
PLE_DIM = 256
GRID_W = 64
SSM_GROUP = 16
SSM_GROUPS = D_MODEL // SSM_GROUP
SSM_STATE = 64
HEAD_DIM = 128
N_HEADS = D_MODEL // HEAD_DIM
WIN_H = 8
WIN_W = 16
D_FF = 5632
EPS = 1e-6

LANES = 128
SUBLANES = 8
VMEM_LIMIT_BYTES = 56 * 1024 * 1024

CHUNK = 16
GROUPS_PER_TILE = LANES // SSM_GROUP
N_LANE_TILES = D_MODEL // LANES
STATE_LANES = GROUPS_PER_TILE * SSM_STATE
CW = CHUNK * LANES
N_LAGS = 2 * CHUNK - 1
PW_ROWS = 128
NEG_BIG = -1e30


def _params(sem, vmem=VMEM_LIMIT_BYTES):
    return pltpu.CompilerParams(dimension_semantics=sem, vmem_limit_bytes=vmem)


def _rmsnorm_bf16(x, gain):
    ms = jnp.mean(x * x, axis=-1, keepdims=True)
    return (x * lax.rsqrt(ms + EPS) * gain).astype(BF16)


def _dot(a, b):
    return jnp.dot(a, b, preferred_element_type=F32)


def _norm_matmul_kernel(x_ref, g_ref, w_ref, o_ref, xn_ref):
    @pl.when(pl.program_id(2) == 0)
    def _():
        xn_ref[...] = _rmsnorm_bf16(x_ref[...], g_ref[...])

    o_ref[...] = _dot(xn_ref[...], w_ref[...]).astype(o_ref.dtype)


def s5_in_proj(h, gain, w, *, n_chunks, tm=512, tn=512):
    hv = h.reshape(n_chunks, CHUNK * D_MODEL)
    n_out = w.shape[1]
    tm = math.gcd(tm, n_chunks)
    return pl.pallas_call(
        _norm_matmul_kernel,
        grid=(n_chunks // tm, CHUNK, n_out // tn),
        in_specs=[
            pl.BlockSpec((tm, D_MODEL), lambda i, s, n: (i, s)),
            pl.BlockSpec((1, D_MODEL), lambda i, s, n: (0, 0)),
            pl.BlockSpec((D_MODEL, tn), lambda i, s, n: (0, n)),
        ],
        out_specs=pl.BlockSpec((None, tm, tn), lambda i, s, n: (s, i, n)),
        out_shape=jax.ShapeDtypeStruct((CHUNK, n_chunks, n_out), F32),
        scratch_shapes=[pltpu.VMEM((tm, D_MODEL), BF16)],
        compiler_params=_params(("parallel", "arbitrary", "arbitrary")),
        name="s5_in_proj",
    )(hv, gain.reshape(1, D_MODEL), w)


def _qkv_kernel(x_ref, g_ref, w_ref, qg_ref, kg_ref, o_ref, xn_ref, *, heads_per_tile, q_tiles):
    n = pl.program_id(1)

    @pl.when(n == 0)
    def _():
        xn_ref[...] = _rmsnorm_bf16(x_ref[...], g_ref[...])

    acc = _dot(xn_ref[...], w_ref[...])

    def head_norm(gain, scale):
        outs = []
        for hh in range(heads_per_tile):
            t = acc[:, hh * HEAD_DIM:(hh + 1) * HEAD_DIM]
            ms = jnp.mean(t * t, axis=-1, keepdims=True)
            outs.append(t * lax.rsqrt(ms + EPS) * gain * scale)
        return jnp.concatenate(outs, axis=-1).astype(o_ref.dtype)

    @pl.when(n < q_tiles)
    def _():
        o_ref[...] = head_norm(qg_ref[...], HEAD_DIM ** -0.5)

    @pl.when((n >= q_tiles) & (n < 2 * q_tiles))
    def _():
        o_ref[...] = head_norm(kg_ref[...], 1.0)

    @pl.when(n >= 2 * q_tiles)
    def _():
        o_ref[...] = acc.astype(o_ref.dtype)


def qkv_proj(h, gain, w, q_gain, k_gain, *, tm=512, tn=512):
    m = h.shape[0]
    n_out = w.shape[1]
    tm = math.gcd(tm, m)
    kern = functools.partial(_qkv_kernel, heads_per_tile=tn // HEAD_DIM,
                             q_tiles=D_MODEL // tn)
    return pl.pallas_call(
        kern,
        grid=(m // tm, n_out // tn),
        in_specs=[
            pl.BlockSpec((tm, D_MODEL), lambda i, n: (i, 0)),
            pl.BlockSpec((1, D_MODEL), lambda i, n: (0, 0)),
            pl.BlockSpec((D_MODEL, tn), lambda i, n: (0, n)),
            pl.BlockSpec((1, HEAD_DIM), lambda i, n: (0, 0)),
            pl.BlockSpec((1, HEAD_DIM), lambda i, n: (0, 0)),
        ],
        out_specs=pl.BlockSpec((tm, tn), lambda i, n: (i, n)),
        out_shape=jax.ShapeDtypeStruct((m, n_out), BF16),
        scratch_shapes=[pltpu.VMEM((tm, D_MODEL), BF16)],
        compiler_params=_params(("parallel", "arbitrary")),
        name="qkv_proj",
    )(h, gain.reshape(1, D_MODEL), w, q_gain.reshape(1, HEAD_DIM), k_gain.reshape(1, HEAD_DIM))


def _ffn_kernel(h_ref, g_ref, wg_ref, wu_ref, wd_ref, o_ref, xn_ref):
    @pl.when(pl.program_id(1) == 0)
    def _():
        x = h_ref[...]
        xn_ref[...] = _rmsnorm_bf16(x, g_ref[...])
        o_ref[...] = x

    xn = xn_ref[...]
    gate = _dot(xn, wg_ref[...])
    up = _dot(xn, wu_ref[...])
    act = (jax.nn.silu(gate) * up).astype(BF16)
    o_ref[...] += _dot(act, wd_ref[...])


def ffn_residual(h, gain, w_gate, w_up, w_down, *, tm=768, tf=512):
    m = h.shape[0]
    tm = math.gcd(tm, m)
    return pl.pallas_call(
        _ffn_kernel,
        grid=(m // tm, D_FF // tf),
        in_specs=[
            pl.BlockSpec((tm, D_MODEL), lambda i, f: (i, 0)),
            pl.BlockSpec((1, D_MODEL), lambda i, f: (0, 0)),
            pl.BlockSpec((D_MODEL, tf), lambda i, f: (0, f)),
            pl.BlockSpec((D_MODEL, tf), lambda i, f: (0, f)),
            pl.BlockSpec((tf, D_MODEL), lambda i, f: (f, 0)),
        ],
        out_specs=pl.BlockSpec((tm, D_MODEL), lambda i, f: (i, 0)),
        out_shape=jax.ShapeDtypeStruct((m, D_MODEL), F32),
        scratch_shapes=[pltpu.VMEM((tm, D_MODEL), BF16)],
        compiler_params=_params(("parallel", "arbitrary")),
        name="ffn_residual",
    )(h, gain.reshape(1, D_MODEL), w_gate, w_up, w_down)


def _ple_kernel(x_ref, g_ref, wg_ref, hres_ref, p_ref, wp_ref, o_ref, xn_ref):
    @pl.when(pl.program_id(1) == 0)
    def _():
        xn_ref[...] = _rmsnorm_bf16(x_ref[...], g_ref[...])

    z = _dot(xn_ref[...], wg_ref[...])
    proj = _dot(p_ref[...], wp_ref[...])
    o_ref[...] = hres_ref[...] + jax.nn.sigmoid(z) * proj


def ple_residual(h, gain, w_gate, p, w_proj, *, tm=512, tn=512):
    m = h.shape[0]
    tm = math.gcd(tm, m)
    return pl.pallas_call(
        _ple_kernel,
        grid=(m // tm, D_MODEL // tn),
        in_specs=[
            pl.BlockSpec((tm, D_MODEL), lambda i, n: (i, 0)),
            pl.BlockSpec((1, D_MODEL), lambda i, n: (0, 0)),
            pl.BlockSpec((D_MODEL, tn), lambda i, n: (0, n)),
            pl.BlockSpec((tm, tn), lambda i, n: (i, n)),
            pl.BlockSpec((tm, PLE_DIM), lambda i, n: (i, 0)),
            pl.BlockSpec((PLE_DIM, tn), lambda i, n: (0, n)),
        ],
        out_specs=pl.BlockSpec((tm, tn), lambda i, n: (i, n)),
        out_shape=jax.ShapeDtypeStruct((m, D_MODEL), F32),
        scratch_shapes=[pltpu.VMEM((tm, D_MODEL), BF16)],
        compiler_params=_params(("parallel", "arbitrary")),
        name="ple_residual",
    )(h, gain.reshape(1, D_MODEL), w_gate, h, p, w_proj)


def _glu_kernel(g_ref, wa_ref, wb_ref, hres_ref, o_ref):
    g = g_ref[...]
    a = _dot(g, wa_ref[...])
    b = _dot(g, wb_ref[...])
    o_ref[...] = hres_ref[...] + a * jax.nn.sigmoid(b)


def s5_glu_residual(g_slot, w_glu, h, *, n_chunks, tm=512, tn=512):
    hv = h.reshape(n_chunks, CHUNK * D_MODEL)
    nt = D_MODEL // tn
    tm = math.gcd(tm, n_chunks)
    out = pl.pallas_call(
        _glu_kernel,
        grid=(n_chunks // tm, CHUNK, nt),
        in_specs=[
            pl.BlockSpec((None, tm, D_MODEL), lambda i, s, n: (s, i, 0)),
            pl.BlockSpec((D_MODEL, tn), lambda i, s, n: (0, n)),
            pl.BlockSpec((D_MODEL, tn), lambda i, s, n: (0, nt + n)),
            pl.BlockSpec((tm, tn), lambda i, s, n: (i, s * nt + n)),
        ],
        out_specs=pl.BlockSpec((tm, tn), lambda i, s, n: (i, s * nt + n)),
        out_shape=jax.ShapeDtypeStruct(hv.shape, F32),
        compiler_params=_params(("parallel", "arbitrary", "arbitrary")),
        name="s5_glu_residual",
    )(g_slot, w_glu, w_glu, hv)
    return out.reshape(h.shape)


def _proj_residual_kernel(a_ref, w_ref, hres_ref, o_ref):
    o_ref[...] = hres_ref[...] + _dot(a_ref[...], w_ref[...])


def attn_out_residual(a, w_o, h, *, tm=512, tn=512):
    m = h.shape[0]
    tm = math.gcd(tm, m)
    return pl.pallas_call(
        _proj_residual_kernel,
        grid=(m // tm, D_MODEL // tn),
        in_specs=[
            pl.BlockSpec((tm, D_MODEL), lambda i, n: (i, 0)),
            pl.BlockSpec((D_MODEL, tn), lambda i, n: (0, n)),
            pl.BlockSpec((tm, tn), lambda i, n: (i, n)),
        ],
        out_specs=pl.BlockSpec((tm, tn), lambda i, n: (i, n)),
        out_shape=jax.ShapeDtypeStruct((m, D_MODEL), F32),
        compiler_params=_params(("parallel", "arbitrary")),
        name="attn_out_residual",
    )(a, w_o, h)


def _attn_plan(rows, r_blk):
    span = r_blk + WIN_H - 1
    n_blk = rows // r_blk
    variants, vid = [], []
    for i in range(n_blk):
        base = min(max(r_blk * i - WIN_H // 2, 0), rows - span)
        key = tuple(min(max(r - WIN_H // 2, 0), rows - WIN_H) - base
                    for r in range(r_blk * i, r_blk * (i + 1))) + (r_blk * i - base,)
        if key not in variants:
            variants.append(key)
        vid.append(variants.index(key))
    return span, variants, np.asarray(vid, np.int32)


def _attn_bias_tables(rpb, rows, r_blk):
    span, variants, vid = _attn_plan(rows, r_blk)
    cols = np.arange(GRID_W)
    col_start = np.clip(cols - WIN_W // 2, 0, GRID_W - WIN_W)
    nq, nk = r_blk * GRID_W, span * GRID_W
    row_idx = np.zeros((len(variants), nq, nk), np.int32)
    col_idx = np.zeros((len(variants), nq, nk), np.int32)
    valid = np.zeros((len(variants), nq, nk), bool)
    kr = np.arange(span)[:, None]
    kc = np.arange(GRID_W)[None, :]
    for v, key in enumerate(variants):
        q0 = key[-1]
        for rl in range(r_blk):
            rs = key[rl]
            for c in range(GRID_W):
                q = rl * GRID_W + c
                ok = (kr >= rs) & (kr < rs + WIN_H) & (kc >= col_start[c]) & (kc < col_start[c] + WIN_W)
                valid[v, q] = ok.reshape(-1)
                row_idx[v, q] = np.broadcast_to(np.clip(kr - (q0 + rl) + WIN_H - 1, 0, 2 * WIN_H - 2),
                                                (span, GRID_W)).reshape(-1)
                col_idx[v, q] = np.broadcast_to(np.clip(kc - c + WIN_W - 1, 0, 2 * WIN_W - 2),
                                                (span, GRID_W)).reshape(-1)
    gathered = rpb[:, row_idx, col_idx].astype(F32)
    return jnp.where(valid[None], gathered, NEG_BIG), vid


def _attn_kernel(vid_ref, q_ref, k_ref, v_ref, b_ref, o_ref, *, rows, r_blk):
    span = r_blk + WIN_H - 1
    nq, nk = r_blk * GRID_W, span * GRID_W

    def body(i, carry):
        base = jnp.clip(r_blk * i - WIN_H // 2, 0, rows - span)
        q0 = pl.multiple_of(i * nq, GRID_W)
        k0 = pl.multiple_of(base * GRID_W, GRID_W)
        q = q_ref[pl.ds(q0, nq), :]
        kb = k_ref[pl.ds(k0, nk), :]
        vb = v_ref[pl.ds(k0, nk), :]
        s = lax.dot_general(q, kb, (((1,), (1,)), ((), ())), preferred_element_type=F32)
        s = s + b_ref[vid_ref[i]]
        m = jnp.max(s, axis=-1, keepdims=True)
        e = jnp.exp(s - m)
        denom = jnp.sum(e, axis=-1, keepdims=True)
        o = _dot(e.astype(BF16), vb) / denom
        o_ref[pl.ds(q0, nq), :] = o.astype(o_ref.dtype)
        return carry

    lax.fori_loop(0, rows // r_blk, body, 0)


def neighborhood_attention(qkv, rpb, *, seq_len, row_block, r_blk=1):
    rows = seq_len // GRID_W
    bias, vid = _attn_bias_tables(rpb, rows, r_blk)
    n_var, nq, nk = bias.shape[1:]
    kern = functools.partial(_attn_kernel, rows=rows, r_blk=r_blk)
    grid_spec = pltpu.PrefetchScalarGridSpec(
        num_scalar_prefetch=1,
        grid=(N_HEADS,),
        in_specs=[
            pl.BlockSpec((seq_len, HEAD_DIM), lambda h, vid: (row_block, h)),
            pl.BlockSpec((seq_len, HEAD_DIM), lambda h, vid: (row_block, N_HEADS + h)),
            pl.BlockSpec((seq_len, HEAD_DIM), lambda h, vid: (row_block, 2 * N_HEADS + h)),
            pl.BlockSpec((None, n_var, nq, nk), lambda h, vid: (h, 0, 0, 0)),
        ],
        out_specs=pl.BlockSpec((seq_len, HEAD_DIM), lambda h, vid: (0, h)),
    )
    return pl.pallas_call(
        kern,
        grid_spec=grid_spec,
        out_shape=jax.ShapeDtypeStruct((seq_len, D_MODEL), BF16),
        compiler_params=_params(("arbitrary",)),
        name="neighborhood_attention",
    )(jnp.asarray(vid), qkv, qkv, qkv, bias)


def _pw_exponents():
    e = np.zeros((PW_ROWS, 1), np.float32)
    for r in range(CHUNK + 1):
        e[r, 0] = r
    for r in range(CHUNK + 1, CHUNK + SUBLANES):
        e[r, 0] = CHUNK * (r - CHUNK + 1)
    for r in range(CHUNK + SUBLANES, CHUNK + 2 * SUBLANES):
        e[r, 0] = CHUNK * (CHUNK + 2 * SUBLANES - r)
    return e


def _s5_prep_kernel(e_ref, prow_ref, b_ref, c_ref, bd_ref, cx_ref, kt_ref, tab_ref):
    expo = e_ref[...]
    row_g = lax.broadcasted_iota(jnp.int32, (LANES, STATE_LANES), 0) // SSM_GROUP
    col_g = lax.broadcasted_iota(jnp.int32, (LANES, STATE_LANES), 1) // SSM_STATE
    b_mask = row_g == col_g
    row_gc = lax.broadcasted_iota(jnp.int32, (STATE_LANES, LANES), 0) // SSM_STATE
    col_gc = lax.broadcasted_iota(jnp.int32, (STATE_LANES, LANES), 1) // SSM_GROUP
    c_mask = row_gc == col_gc
    sub = lax.broadcasted_iota(jnp.int32, (SUBLANES, STATE_LANES), 0)

    k_same = None
    for d in range(2):
        a_re = prow_ref[d, 0]
        a_im = prow_ref[d, 1]
        dt = jnp.exp(prow_ref[d, 2])
        mag = jnp.exp(expo * (a_re * dt))
        ang = expo * (a_im * dt)
        pw_re = mag * jnp.cos(ang)
        pw_im = mag * jnp.sin(ang)
        pw_re_t = pw_re.T
        pw_im_t = pw_im.T

        xr = pw_re[1:2] - 1.0
        xi = pw_im[1:2]
        den = a_re * a_re + a_im * a_im
        f_re = (xr * a_re + xi * a_im) / den
        f_im = (xi * a_re - xr * a_im) / den
        b_re = jnp.where(b_mask, b_ref[d, 0], 0.0)
        b_im = jnp.where(b_mask, b_ref[d, 1], 0.0)
        bb_re = f_re * b_re - f_im * b_im
        bb_im = f_re * b_im + f_im * b_re
        c_re = jnp.where(c_mask, c_ref[d, 0], 0.0)
        c_im = jnp.where(c_mask, c_ref[d, 1], 0.0)
        c_cat = jnp.concatenate([c_re, -c_im], axis=0)

        col0 = 2 * d * STATE_LANES
        for s in range(CHUNK):
            n_in = CHUNK - 1 - s if d == 0 else s
            pr, pi = pw_re[n_in:n_in + 1], pw_im[n_in:n_in + 1]
            blk_re = bb_re * pr - bb_im * pi
            blk_im = bb_re * pi + bb_im * pr
            rows = slice(s * LANES, (s + 1) * LANES)
            bd_ref[rows, col0:col0 + STATE_LANES] = blk_re.astype(bd_ref.dtype)
            bd_ref[rows, col0 + STATE_LANES:col0 + 2 * STATE_LANES] = blk_im.astype(bd_ref.dtype)

            k_lag = jnp.dot(jnp.concatenate([blk_re, blk_im], axis=1), c_cat,
                            precision=lax.Precision.HIGHEST, preferred_element_type=F32)
            if n_in == 0:
                k_same = k_lag if k_same is None else k_same + k_lag
            else:
                lag = n_in if d == 0 else -n_in
                kt_ref[CHUNK - 1 + lag] = k_lag.astype(kt_ref.dtype)

            n_out = s + 1 if d == 0 else CHUNK - s
            qr = pw_re_t[:, n_out:n_out + 1]
            qi = pw_im_t[:, n_out:n_out + 1]
            cols = slice(s * LANES, (s + 1) * LANES)
            cx_ref[col0:col0 + STATE_LANES, cols] = (c_re * qr - c_im * qi).astype(cx_ref.dtype)
            cx_ref[col0 + STATE_LANES:col0 + 2 * STATE_LANES, cols] = (
                -(c_re * qi + c_im * qr)).astype(cx_ref.dtype)

        for k, sh in enumerate((1, 2, 4)):
            r = CHUNK + sh - 1 if sh < 4 else CHUNK + 3
            keep = (sub >= sh) if d == 0 else (sub < SUBLANES - sh)
            tab_ref[d, 2 * k] = jnp.where(keep, pw_re[r:r + 1], 0.0)
            tab_ref[d, 2 * k + 1] = jnp.where(keep, pw_im[r:r + 1], 0.0)
        p0 = CHUNK if d == 0 else CHUNK + SUBLANES
        tab_ref[d, 6] = pw_re[p0:p0 + SUBLANES]
        tab_ref[d, 7] = pw_im[p0:p0 + SUBLANES]
    kt_ref[CHUNK - 1] = k_same.astype(kt_ref.dtype)


def _tile_params(x):
    x = x.reshape(2, N_LANE_TILES, 1, STATE_LANES)
    return jnp.transpose(x, (1, 0, 2, 3))


def s5_prepare(a_re, a_im, log_dt, b_re, b_im, c_re, c_im):
    g, p, gc = SSM_GROUPS, SSM_STATE, SSM_GROUP
    prow = jnp.stack([_tile_params(a_re), _tile_params(a_im),
                      _tile_params(jnp.broadcast_to(log_dt[:, :, None], (2, g, p)))], axis=2)

    def b_tiles(b):
        b = b.reshape(2, N_LANE_TILES, GROUPS_PER_TILE, p, gc)
        b = jnp.transpose(b, (1, 0, 4, 2, 3)).reshape(N_LANE_TILES, 2, 1, gc, STATE_LANES)
        return jnp.broadcast_to(b, (N_LANE_TILES, 2, GROUPS_PER_TILE, gc, STATE_LANES)).reshape(
            N_LANE_TILES, 2, LANES, STATE_LANES)

    def c_tiles(c):
        c = c.reshape(2, N_LANE_TILES, GROUPS_PER_TILE, gc, p)
        c = jnp.transpose(c, (1, 0, 2, 4, 3)).reshape(N_LANE_TILES, 2, STATE_LANES, 1, gc)
        return jnp.broadcast_to(c, (N_LANE_TILES, 2, STATE_LANES, GROUPS_PER_TILE, gc)).reshape(
            N_LANE_TILES, 2, STATE_LANES, LANES)

    b_t = jnp.stack([b_tiles(b_re), b_tiles(b_im)], axis=2)
    c_t = jnp.stack([c_tiles(c_re), c_tiles(c_im)], axis=2)
    expo = jnp.asarray(_pw_exponents())

    return pl.pallas_call(
        _s5_prep_kernel,
        grid=(N_LANE_TILES,),
        in_specs=[
            pl.BlockSpec((PW_ROWS, 1), lambda j: (0, 0)),
            pl.BlockSpec((None, 2, 3, 1, STATE_LANES), lambda j: (j, 0, 0, 0, 0)),
            pl.BlockSpec((None, 2, 2, LANES, STATE_LANES), lambda j: (j, 0, 0, 0, 0)),
            pl.BlockSpec((None, 2, 2, STATE_LANES, LANES), lambda j: (j, 0, 0, 0, 0)),
        ],
        out_specs=[
            pl.BlockSpec((None, CW, 4 * STATE_LANES), lambda j: (j, 0, 0)),
            pl.BlockSpec((None, 4 * STATE_LANES, CW), lambda j: (j, 0, 0)),
            pl.BlockSpec((None, N_LAGS, LANES, LANES), lambda j: (j, 0, 0, 0)),
            pl.BlockSpec((None, 2, 8, SUBLANES, STATE_LANES), lambda j: (j, 0, 0, 0, 0)),
        ],
        out_shape=[
            jax.ShapeDtypeStruct((N_LANE_TILES, CW, 4 * STATE_LANES), BF16),
            jax.ShapeDtypeStruct((N_LANE_TILES, 4 * STATE_LANES, CW), BF16),
            jax.ShapeDtypeStruct((N_LANE_TILES, N_LAGS, LANES, LANES), BF16),
            jax.ShapeDtypeStruct((N_LANE_TILES, 2, 8, SUBLANES, STATE_LANES), F32),
        ],
        compiler_params=_params(("arbitrary",)),
        name="s5_prepare",
    )(expo, prow, b_t, c_t)


def _cmul_add(xr, xi, ar, ai, yr, yi):
    return xr + ar * yr - ai * yi, xi + ar * yi + ai * yr


def _scan_sequence(x_ref, tab_ref, row0, n_rows):
    sl = STATE_LANES
    n_blk = n_rows // SUBLANES
    sub = lax.broadcasted_iota(jnp.int32, (SUBLANES, sl), 0)

    def one_block(d, row, carry):
        c0 = 2 * d * sl
        xr = x_ref[pl.ds(row, SUBLANES), c0:c0 + sl]
        xi = x_ref[pl.ds(row, SUBLANES), c0 + sl:c0 + 2 * sl]
        for k, sh in enumerate((1, 2, 4)):
            shift = sh if d == 0 else SUBLANES - sh
            yr = pltpu.roll(xr, shift, 0)
            yi = pltpu.roll(xi, shift, 0)
            xr, xi = _cmul_add(xr, xi, tab_ref[d, 2 * k], tab_ref[d, 2 * k + 1], yr, yi)
        cr = jnp.broadcast_to(carry[0], (SUBLANES, sl))
        ci = jnp.broadcast_to(carry[1], (SUBLANES, sl))
        xr, xi = _cmul_add(xr, xi, tab_ref[d, 6], tab_ref[d, 7], cr, ci)
        shift, vacated = (1, 0) if d == 0 else (SUBLANES - 1, SUBLANES - 1)
        x_ref[pl.ds(row, SUBLANES), c0:c0 + sl] = jnp.where(
            sub == vacated, cr, pltpu.roll(xr, shift, 0))
        x_ref[pl.ds(row, SUBLANES), c0 + sl:c0 + 2 * sl] = jnp.where(
            sub == vacated, ci, pltpu.roll(xi, shift, 0))
        edge = SUBLANES - 1 if d == 0 else 0
        return xr[edge:edge + 1], xi[edge:edge + 1]

    def body(i, carry):
        f_row = pl.multiple_of(row0 + i * SUBLANES, SUBLANES)
        b_row = pl.multiple_of(row0 + (n_blk - 1 - i) * SUBLANES, SUBLANES)
        cf = one_block(0, f_row, carry[0:2])
        cb = one_block(1, b_row, carry[2:4])
        return cf + cb

    z = jnp.zeros((1, sl), F32)
    lax.fori_loop(0, n_blk, body, (z, z, z, z))


def _s5_core_kernel(u_ref, bd_ref, cx_ref, kt_ref, tab_ref, d_ref, o_ref,
                    lhs_ref, x_ref, m_ref, *, bk, seq_chunks):
    phase = pl.program_id(1)
    cb = pl.program_id(2)
    n_cb = pl.num_programs(2)
    off = pl.multiple_of(cb * bk, bk)
    starts = [sum(seq_chunks[:k]) for k in range(len(seq_chunks))]

    for s in range(CHUNK):
        lhs_ref[:, s * LANES:(s + 1) * LANES] = u_ref[s].astype(BF16)

    @pl.when(phase == 0)
    def _():
        x_ref[pl.ds(off, bk), :] = _dot(lhs_ref[...], bd_ref[...])

        @pl.when(cb == n_cb - 1)
        def _():
            for st, n in zip(starts, seq_chunks):
                _scan_sequence(x_ref, tab_ref, st, n)

    @pl.when(phase == 1)
    def _():
        @pl.when(cb == 0)
        def _():
            for s in range(CHUNK):
                for t in range(CHUNK):
                    m_ref[s * LANES:(s + 1) * LANES, t * LANES:(t + 1) * LANES] = (
                        kt_ref[CHUNK - 1 + t - s])

        y = _dot(lhs_ref[...], m_ref[...])
        y = y + _dot(x_ref[pl.ds(off, bk), :].astype(BF16), cx_ref[...])
        dvec = d_ref[...]
        for t in range(CHUNK):
            yt = y[:, t * LANES:(t + 1) * LANES] + dvec * u_ref[t]
            o_ref[t] = jax.nn.gelu(yt).astype(o_ref.dtype)


def s5_core(u_slot, bd, cx, kt, tab, d, *, seq_chunks, bk=128):
    n_chunks = u_slot.shape[1]
    bk = math.gcd(bk, *seq_chunks)
    assert n_chunks == sum(seq_chunks)
    x_rows = n_chunks
    kern = functools.partial(_s5_core_kernel, bk=bk, seq_chunks=tuple(seq_chunks))
    single = pl.Buffered(1)
    return pl.pallas_call(
        kern,
        grid=(N_LANE_TILES, 2, n_chunks // bk),
        in_specs=[
            pl.BlockSpec((CHUNK, bk, LANES), lambda j, ph, c: (0, c, j)),
            pl.BlockSpec((None, CW, 4 * STATE_LANES), lambda j, ph, c: (j, 0, 0),
                         pipeline_mode=single),
            pl.BlockSpec((None, 4 * STATE_LANES, CW), lambda j, ph, c: (j, 0, 0),
                         pipeline_mode=single),
            pl.BlockSpec((None, N_LAGS, LANES, LANES), lambda j, ph, c: (j, 0, 0, 0)),
            pl.BlockSpec((None, 2, 8, SUBLANES, STATE_LANES), lambda j, ph, c: (j, 0, 0, 0, 0)),
            pl.BlockSpec((1, LANES), lambda j, ph, c: (0, j)),
        ],
        out_specs=pl.BlockSpec((CHUNK, bk, LANES), lambda j, ph, c: (0, c * ph, j)),
        out_shape=jax.ShapeDtypeStruct((CHUNK, n_chunks, D_MODEL), BF16),
        scratch_shapes=[
            pltpu.VMEM((bk, CW), BF16),
            pltpu.VMEM((x_rows, 4 * STATE_LANES), F32),
            pltpu.VMEM((CW, CW), BF16),
        ],
        compiler_params=_params(("arbitrary", "arbitrary", "arbitrary")),
        name="s5_core",
    )(u_slot, bd, cx, kt, tab, d.reshape(1, D_MODEL))


def _trunk(h, p_all, seq_lens, norm_mix, norm_ffn, norm_ple, s5_w_in, s5_a_re, s5_a_im,
           s5_log_dt, s5_b_re, s5_b_im, s5_c_re, s5_c_im, s5_d, s5_w_glu, attn_w_qkv,
           attn_q_norm, attn_k_norm, attn_rpb, attn_w_o, ffn_w_gate, ffn_w_up, ffn_w_down,
           ple_w_gate, ple_w_proj):
    m = h.shape[0]
    n_chunks = m // CHUNK
    seq_chunks = tuple(n // CHUNK for n in seq_lens)
    for i in range(DEPTH):
        j = i // 2
        if i % 2 == 0:
            bd, cx, kt, tab = s5_prepare(s5_a_re[j], s5_a_im[j], s5_log_dt[j], s5_b_re[j],
                                         s5_b_im[j], s5_c_re[j], s5_c_im[j])
            u_slot = s5_in_proj(h, norm_mix[i], s5_w_in[j].astype(BF16), n_chunks=n_chunks)
            g_slot = s5_core(u_slot, bd, cx, kt, tab, s5_d[j], seq_chunks=seq_chunks)
            h = s5_glu_residual(g_slot, s5_w_glu[j].astype(BF16), h, n_chunks=n_chunks)
        else:
            qkv = qkv_proj(h, norm_mix[i], attn_w_qkv[j].astype(BF16), attn_q_norm[j],
                           attn_k_norm[j])
            outs, row0 = [], 0
            for n in seq_lens:
                assert row0 % n == 0
                outs.append(neighborhood_attention(qkv, attn_rpb[j], seq_len=n,
                                                   row_block=row0 // n))
                row0 += n
            h = attn_out_residual(jnp.concatenate(outs, axis=0), attn_w_o[j].astype(BF16), h)
        h = ffn_residual(h, norm_ffn[i], ffn_w_gate[i].astype(BF16), ffn_w_up[i].astype(BF16),
                         ffn_w_down[i].astype(BF16))
        h = ple_residual(h, norm_ple[i], ple_w_gate[i].astype(BF16), p_all[i],
                         ple_w_proj[i].astype(BF16))
    return h


def kernel(x_prompt, x_sample, p_prompt, p_sample, norm_mix, norm_ffn, norm_ple, s5_w_in, s5_a_re, s5_a_im, s5_log_dt, s5_b_re, s5_b_im, s5_c_re, s5_c_im, s5_d, s5_w_glu, attn_w_qkv, attn_q_norm, attn_k_norm, attn_rpb, attn_w_o, ffn_w_gate, ffn_w_up, ffn_w_down, ple_w_gate, ple_w_proj):
    assert x_prompt.shape[0] == 1 and x_sample.shape[0] == 1
    n_s, n_p = x_sample.shape[1], x_prompt.shape[1]
    h = jnp.concatenate([x_sample[0], x_prompt[0]], axis=0)
    p_all = jnp.concatenate([p_sample[:, 0], p_prompt[:, 0]], axis=1).astype(BF16)
    h = _trunk(h, p_all, (n_s, n_p), norm_mix, norm_ffn, norm_ple, s5_w_in, s5_a_re, s5_a_im,
               s5_log_dt, s5_b_re, s5_b_im, s5_c_re, s5_c_im, s5_d, s5_w_glu, attn_w_qkv,
               attn_q_norm, attn_k_norm, attn_rpb, attn_w_o, ffn_w_gate, ffn_w_up, ffn_w_down,
               ple_w_gate, ple_w_proj)
    return h[n_s:][None], h[:n_s][None]
```

```python
import functools
import math

import numpy as np
import jax
import jax.numpy as jnp
from jax import lax
from jax.experimental import pallas as pl
from jax.experimental.pallas import tpu as pltpu

F32 = jnp.float32
BF16 = jnp.bfloat16

D_MODEL = 2048
DEPTH = 4
PLE_DIM = 256
GRID_W = 64
SSM_GROUP = 16
SSM_GROUPS = D_MODEL // SSM_GROUP
SSM_STATE = 64
HEAD_DIM = 128
N_HEADS = D_MODEL // HEAD_DIM
WIN_H = 8
WIN_W = 16
D_FF = 5632
EPS = 1e-6

LANES = 128
SUBLANES = 8
VMEM_LIMIT_BYTES = 56 * 1024 * 1024

TN = 512
CHUNK = 16
GROUPS_PER_TILE = LANES // SSM_GROUP
N_LANE_TILES = D_MODEL // LANES
STATE_LANES = GROUPS_PER_TILE * SSM_STATE
CW = CHUNK * LANES
N_LAGS = 2 * CHUNK - 1
PW_ROWS = 128
NEG_BIG = -1e30


def _params(sem, vmem=VMEM_LIMIT_BYTES):
    return pltpu.CompilerParams(dimension_semantics=sem, vmem_limit_bytes=vmem)


def _rmsnorm_bf16(x, gain):
    ms = jnp.mean(x * x, axis=-1, keepdims=True)
    return (x * lax.rsqrt(ms + EPS) * gain).astype(BF16)


def _dot(a, b):
    return jnp.dot(a, b, preferred_element_type=F32)


def _resident(shape):
    return pl.BlockSpec(shape, lambda *_: (0,) * len(shape), pipeline_mode=pl.Buffered(1))


def _col_chunks(width):
    return [slice(n * TN, (n + 1) * TN) for n in range(width // TN)]


def _in_proj_kernel(x_ref, g_ref, w_ref, o_ref, xn_ref):
    xn_ref[...] = _rmsnorm_bf16(x_ref[...], g_ref[...])
    for cols in _col_chunks(D_MODEL):
        o_ref[:, cols] = _dot(xn_ref[...], w_ref[:, cols])


def s5_in_proj(h, gain, w, *, tm=512):
    m = h.shape[0]
    tm = math.gcd(tm, m)
    return pl.pallas_call(
        _in_proj_kernel,
        grid=(m // tm,),
        in_specs=[
            pl.BlockSpec((tm, D_MODEL), lambda i: (i, 0)),
            _resident((1, D_MODEL)),
            _resident((D_MODEL, D_MODEL)),
        ],
        out_specs=pl.BlockSpec((tm, D_MODEL), lambda i: (i, 0)),
        out_shape=jax.ShapeDtypeStruct((m, D_MODEL), F32),
        scratch_shapes=[pltpu.VMEM((tm, D_MODEL), BF16)],
        compiler_params=_params(("parallel",)),
        name="s5_in_proj",
    )(h, gain.reshape(1, D_MODEL), w)


def _qkv_kernel(x_ref, g_ref, w_ref, qg_ref, kg_ref, o_ref, xn_ref, *, heads_per_tile, q_tiles):
    n = pl.program_id(1)

    @pl.when(n == 0)
    def _():
        xn_ref[...] = _rmsnorm_bf16(x_ref[...], g_ref[...])

    acc = _dot(xn_ref[...], w_ref[...])

    def head_norm(gain, scale):
        outs = []
        for hh in range(heads_per_tile):
            t = acc[:, hh * HEAD_DIM:(hh + 1) * HEAD_DIM]
            ms = jnp.mean(t * t, axis=-1, keepdims=True)
            outs.append(t * lax.rsqrt(ms + EPS) * gain * scale)
        return jnp.concatenate(outs, axis=-1).astype(o_ref.dtype)

    @pl.when(n < q_tiles)
    def _():
        o_ref[...] = head_norm(qg_ref[...], HEAD_DIM ** -0.5)

    @pl.when((n >= q_tiles) & (n < 2 * q_tiles))
    def _():
        o_ref[...] = head_norm(kg_ref[...], 1.0)

    @pl.when(n >= 2 * q_tiles)
    def _():
        o_ref[...] = acc.astype(o_ref.dtype)


def qkv_proj(h, gain, w, q_gain, k_gain, *, tm=1024):
    m = h.shape[0]
    n_out = w.shape[1]
    tm = math.gcd(tm, m)
    kern = functools.partial(_qkv_kernel, heads_per_tile=TN // HEAD_DIM,
                             q_tiles=D_MODEL // TN)
    return pl.pallas_call(
        kern,
        grid=(m // tm, n_out // TN),
        in_specs=[
            pl.BlockSpec((tm, D_MODEL), lambda i, n: (i, 0)),
            pl.BlockSpec((1, D_MODEL), lambda i, n: (0, 0)),
            pl.BlockSpec((D_MODEL, TN), lambda i, n: (0, n)),
            pl.BlockSpec((1, HEAD_DIM), lambda i, n: (0, 0)),
            pl.BlockSpec((1, HEAD_DIM), lambda i, n: (0, 0)),
        ],
        out_specs=pl.BlockSpec((tm, TN), lambda i, n: (i, n)),
        out_shape=jax.ShapeDtypeStruct((m, n_out), BF16),
        scratch_shapes=[pltpu.VMEM((tm, D_MODEL), BF16)],
        compiler_params=_params(("parallel", "arbitrary")),
        name="qkv_proj",
    )(h, gain.reshape(1, D_MODEL), w, q_gain.reshape(1, HEAD_DIM), k_gain.reshape(1, HEAD_DIM))


def _ffn_kernel(h_ref, g_ref, wg_ref, wu_ref, wd_ref, o_ref, xn_ref):
    @pl.when(pl.program_id(1) == 0)
    def _():
        x = h_ref[...]
        xn_ref[...] = _rmsnorm_bf16(x, g_ref[...])
        o_ref[...] = x

    xn = xn_ref[...]
    gate = _dot(xn, wg_ref[...])
    up = _dot(xn, wu_ref[...])
    act = (jax.nn.silu(gate) * up).astype(BF16)
    o_ref[...] += _dot(act, wd_ref[...])


def ffn_residual(h, gain, w_gate, w_up, w_down, *, tm=768, tf=512):
    m = h.shape[0]
    tm = math.gcd(tm, m)
    return pl.pallas_call(
        _ffn_kernel,
        grid=(m // tm, D_FF // tf),
        in_specs=[
            pl.BlockSpec((tm, D_MODEL), lambda i, f: (i, 0)),
            pl.BlockSpec((1, D_MODEL), lambda i, f: (0, 0)),
            pl.BlockSpec((D_MODEL, tf), lambda i, f: (0, f)),
            pl.BlockSpec((D_MODEL, tf), lambda i, f: (0, f)),
            pl.BlockSpec((tf, D_MODEL), lambda i, f: (f, 0)),
        ],
        out_specs=pl.BlockSpec((tm, D_MODEL), lambda i, f: (i, 0)),
        out_shape=jax.ShapeDtypeStruct((m, D_MODEL), F32),
        scratch_shapes=[pltpu.VMEM((tm, D_MODEL), BF16)],
        compiler_params=_params(("parallel", "arbitrary")),
        name="ffn_residual",
    )(h, gain.reshape(1, D_MODEL), w_gate, w_up, w_down)


def _ple_kernel(x_ref, g_ref, wg_ref, p_ref, wp_ref, o_ref, xn_ref):
    xn_ref[...] = _rmsnorm_bf16(x_ref[...], g_ref[...])
    for cols in _col_chunks(D_MODEL):
        z = _dot(xn_ref[...], wg_ref[:, cols])
        proj = _dot(p_ref[...], wp_ref[:, cols])
        o_ref[:, cols] = x_ref[:, cols] + jax.nn.sigmoid(z) * proj


def ple_residual(h, gain, w_gate, p, w_proj, *, tm=512):
    m = h.shape[0]
    tm = math.gcd(tm, m)
    return pl.pallas_call(
        _ple_kernel,
        grid=(m // tm,),
        in_specs=[
            pl.BlockSpec((tm, D_MODEL), lambda i: (i, 0)),
            _resident((1, D_MODEL)),
            _resident((D_MODEL, D_MODEL)),
            pl.BlockSpec((tm, PLE_DIM), lambda i: (i, 0)),
            _resident((PLE_DIM, D_MODEL)),
        ],
        out_specs=pl.BlockSpec((tm, D_MODEL), lambda i: (i, 0)),
        out_shape=jax.ShapeDtypeStruct((m, D_MODEL), F32),
        scratch_shapes=[pltpu.VMEM((tm, D_MODEL), BF16)],
        compiler_params=_params(("parallel",)),
        name="ple_residual",
    )(h, gain.reshape(1, D_MODEL), w_gate, p, w_proj)


def _glu_kernel(g_ref, w_ref, h_ref, o_ref):
    for n, cols in enumerate(_col_chunks(D_MODEL)):
        gate_cols = slice(D_MODEL + n * TN, D_MODEL + (n + 1) * TN)
        a = _dot(g_ref[...], w_ref[:, cols])
        b = _dot(g_ref[...], w_ref[:, gate_cols])
        o_ref[:, cols] = h_ref[:, cols] + a * jax.nn.sigmoid(b)


def s5_glu_residual(g, w_glu, h, *, tm=512):
    m = h.shape[0]
    tm = math.gcd(tm, m)
    return pl.pallas_call(
        _glu_kernel,
        grid=(m // tm,),
        in_specs=[
            pl.BlockSpec((tm, D_MODEL), lambda i: (i, 0)),
            _resident((D_MODEL, 2 * D_MODEL)),
            pl.BlockSpec((tm, D_MODEL), lambda i: (i, 0)),
        ],
        out_specs=pl.BlockSpec((tm, D_MODEL), lambda i: (i, 0)),
        out_shape=jax.ShapeDtypeStruct((m, D_MODEL), F32),
        compiler_params=_params(("parallel",)),
        name="s5_glu_residual",
    )(g, w_glu, h)


def _proj_residual_kernel(a_ref, w_ref, h_ref, o_ref):
    for cols in _col_chunks(D_MODEL):
        o_ref[:, cols] = h_ref[:, cols] + _dot(a_ref[...], w_ref[:, cols])


def attn_out_residual(a, w_o, h, *, tm=512):
    m = h.shape[0]
    tm = math.gcd(tm, m)
    return pl.pallas_call(
        _proj_residual_kernel,
        grid=(m // tm,),
        in_specs=[
            pl.BlockSpec((tm, D_MODEL), lambda i: (i, 0)),
            _resident((D_MODEL, D_MODEL)),
            pl.BlockSpec((tm, D_MODEL), lambda i: (i, 0)),
        ],
        out_specs=pl.BlockSpec((tm, D_MODEL), lambda i: (i, 0)),
        out_shape=jax.ShapeDtypeStruct((m, D_MODEL), F32),
        compiler_params=_params(("parallel",)),
        name="attn_out_residual",
    )(a, w_o, h)


def _attn_plan(rows, r_blk):
    span = r_blk + WIN_H - 1
    variants, vid = [], []
    for i in range(rows // r_blk):
        base = min(max(r_blk * i - WIN_H // 2, 0), rows - span)
        key = tuple(min(max(r - WIN_H // 2, 0), rows - WIN_H) - base
                    for r in range(r_blk * i, r_blk * (i + 1))) + (r_blk * i - base,)
        if key not in variants:
            variants.append(key)
        vid.append(variants.index(key))
    return span, variants, np.asarray(vid, np.int32)


def _attn_bias_tables(rpb, rows, r_blk):
    span, variants, vid = _attn_plan(rows, r_blk)
    cols = np.arange(GRID_W)
    col_start = np.clip(cols - WIN_W // 2, 0, GRID_W - WIN_W)
    in_win = (cols[None, :] >= col_start[:, None]) & (cols[None, :] < col_start[:, None] + WIN_W)
    pad = GRID_W - WIN_W
    rp = jnp.pad(rpb.astype(F32), ((0, 0), (0, 0), (pad, pad)))
    band = jnp.stack([rp[:, :, GRID_W - 1 - c:2 * GRID_W - 1 - c] for c in range(GRID_W)], axis=2)
    band = jnp.where(in_win[None, None], band, NEG_BIG)
    outside = jnp.full((N_HEADS, GRID_W, GRID_W), NEG_BIG, F32)
    tables = []
    for key in variants:
        q0 = key[-1]
        row_blocks = []
        for rl in range(r_blk):
            rs = key[rl]
            pieces = [band[:, kr - (q0 + rl) + WIN_H - 1] if rs <= kr < rs + WIN_H else outside
                      for kr in range(span)]
            row_blocks.append(jnp.concatenate(pieces, axis=-1))
        tables.append(jnp.concatenate(row_blocks, axis=1))
    return jnp.stack(tables, axis=1), vid


def _attn_kernel(vid_ref, q_ref, k_ref, v_ref, b_ref, *rest, rows, r_blk, unroll):
    o_ref = rest[-1]
    span = r_blk + WIN_H - 1
    nq, nk = r_blk * GRID_W, span * GRID_W

    def body(i, carry):
        base = jnp.clip(r_blk * i - WIN_H // 2, 0, rows - span)
        q0 = pl.multiple_of(i * nq, GRID_W)
        k0 = pl.multiple_of(base * GRID_W, GRID_W)
        q = q_ref[pl.ds(q0, nq), :]
        kb = k_ref[pl.ds(k0, nk), :]
        vb = v_ref[pl.ds(k0, nk), :]
        s = lax.dot_general(q, kb, (((1,), (1,)), ((), ())), preferred_element_type=F32)
        s = s + b_ref[vid_ref[i]]
        m = jnp.max(s, axis=-1, keepdims=True)
        e = jnp.exp(s - m)
        denom = jnp.sum(e, axis=-1, keepdims=True)
        o = _dot(e.astype(BF16), vb) / denom
        o_ref[pl.ds(q0, nq), :] = o.astype(o_ref.dtype)
        return carry

    lax.fori_loop(0, rows // r_blk, body, 0, unroll=unroll)


def neighborhood_attention(qkv, rpb, prev_out, *, seq_len, row_block, r_blk=2, unroll=8):
    rows = seq_len // GRID_W
    bias, vid = _attn_bias_tables(rpb, rows, r_blk)
    n_var, nq, nk = bias.shape[1:]
    n_blk = rows // r_blk
    kern = functools.partial(_attn_kernel, rows=rows, r_blk=r_blk,
                             unroll=math.gcd(unroll, n_blk))
    in_specs = [
        pl.BlockSpec((seq_len, HEAD_DIM), lambda h, vid: (row_block, h)),
        pl.BlockSpec((seq_len, HEAD_DIM), lambda h, vid: (row_block, N_HEADS + h)),
        pl.BlockSpec((seq_len, HEAD_DIM), lambda h, vid: (row_block, 2 * N_HEADS + h)),
        pl.BlockSpec((None, n_var, nq, nk), lambda h, vid: (h, 0, 0, 0)),
    ]
    args = [jnp.asarray(vid), qkv, qkv, qkv, bias]
    aliases = {}
    if prev_out is not None:
        in_specs.append(pl.BlockSpec(memory_space=pl.ANY))
        args.append(prev_out)
        aliases = {len(args) - 1: 0}
    grid_spec = pltpu.PrefetchScalarGridSpec(
        num_scalar_prefetch=1,
        grid=(N_HEADS,),
        in_specs=in_specs,
        out_specs=pl.BlockSpec((seq_len, HEAD_DIM), lambda h, vid: (row_block, h)),
    )
    return pl.pallas_call(
        kern,
        grid_spec=grid_spec,
        out_shape=jax.ShapeDtypeStruct((qkv.shape[0], D_MODEL), BF16),
        input_output_aliases=aliases,
        compiler_params=_params(("arbitrary",)),
        name="neighborhood_attention",
    )(*args)


def _pw_exponents():
    e = np.zeros((PW_ROWS, 1), np.float32)
    for r in range(CHUNK + 1):
        e[r, 0] = r
    for r in range(CHUNK + 1, CHUNK + SUBLANES):
        e[r, 0] = CHUNK * (r - CHUNK + 1)
    for r in range(CHUNK + SUBLANES, CHUNK + 2 * SUBLANES):
        e[r, 0] = CHUNK * (CHUNK + 2 * SUBLANES - r)
    return e


def _s5_prep_kernel(e_ref, prow_ref, b_ref, c_ref, bd_ref, cx_ref, kt_ref, tab_ref):
    expo = e_ref[...]
    row_g = lax.broadcasted_iota(jnp.int32, (LANES, STATE_LANES), 0) // SSM_GROUP
    col_g = lax.broadcasted_iota(jnp.int32, (LANES, STATE_LANES), 1) // SSM_STATE
    b_mask = row_g == col_g
    row_gc = lax.broadcasted_iota(jnp.int32, (STATE_LANES, LANES), 0) // SSM_STATE
    col_gc = lax.broadcasted_iota(jnp.int32, (STATE_LANES, LANES), 1) // SSM_GROUP
    c_mask = row_gc == col_gc
    sub = lax.broadcasted_iota(jnp.int32, (SUBLANES, STATE_LANES), 0)

    k_same = None
    for d in range(2):
        a_re = prow_ref[d, 0]
        a_im = prow_ref[d, 1]
        dt = jnp.exp(prow_ref[d, 2])
        mag = jnp.exp(expo * (a_re * dt))
        ang = expo * (a_im * dt)
        pw_re = mag * jnp.cos(ang)
        pw_im = mag * jnp.sin(ang)
        pw_re_t = pw_re.T
        pw_im_t = pw_im.T

        xr = pw_re[1:2] - 1.0
        xi = pw_im[1:2]
        den = a_re * a_re + a_im * a_im
        f_re = (xr * a_re + xi * a_im) / den
        f_im = (xi * a_re - xr * a_im) / den
        b_re = jnp.where(b_mask, b_ref[d, 0], 0.0)
        b_im = jnp.where(b_mask, b_ref[d, 1], 0.0)
        bb_re = f_re * b_re - f_im * b_im
        bb_im = f_re * b_im + f_im * b_re
        c_re = jnp.where(c_mask, c_ref[d, 0], 0.0)
        c_im = jnp.where(c_mask, c_ref[d, 1], 0.0)
        c_cat = jnp.concatenate([c_re, -c_im], axis=0)

        col0 = 2 * d * STATE_LANES
        for s in range(CHUNK):
            n_in = CHUNK - 1 - s if d == 0 else s
            pr, pi = pw_re[n_in:n_in + 1], pw_im[n_in:n_in + 1]
            blk_re = bb_re * pr - bb_im * pi
            blk_im = bb_re * pi + bb_im * pr
            rows = slice(s * LANES, (s + 1) * LANES)
            bd_ref[rows, col0:col0 + STATE_LANES] = blk_re.astype(bd_ref.dtype)
            bd_ref[rows, col0 + STATE_LANES:col0 + 2 * STATE_LANES] = blk_im.astype(bd_ref.dtype)

            k_lag = jnp.dot(jnp.concatenate([blk_re, blk_im], axis=1), c_cat,
                            precision=lax.Precision.HIGHEST, preferred_element_type=F32)
            if n_in == 0:
                k_same = k_lag if k_same is None else k_same + k_lag
            else:
                lag = n_in if d == 0 else -n_in
                kt_ref[CHUNK - 1 + lag] = k_lag.astype(kt_ref.dtype)

            n_out = s + 1 if d == 0 else CHUNK - s
            qr = pw_re_t[:, n_out:n_out + 1]
            qi = pw_im_t[:, n_out:n_out + 1]
            cols = slice(s * LANES, (s + 1) * LANES)
            cx_ref[col0:col0 + STATE_LANES, cols] = (c_re * qr - c_im * qi).astype(cx_ref.dtype)
            cx_ref[col0 + STATE_LANES:col0 + 2 * STATE_LANES, cols] = (
                -(c_re * qi + c_im * qr)).astype(cx_ref.dtype)

        for k, sh in enumerate((1, 2, 4)):
            r = CHUNK + sh - 1 if sh < 4 else CHUNK + 3
            keep = (sub >= sh) if d == 0 else (sub < SUBLANES - sh)
            tab_ref[d, 2 * k] = jnp.where(keep, pw_re[r:r + 1], 0.0)
            tab_ref[d, 2 * k + 1] = jnp.where(keep, pw_im[r:r + 1], 0.0)
        p0 = CHUNK if d == 0 else CHUNK + SUBLANES
        tab_ref[d, 6] = pw_re[p0:p0 + SUBLANES]
        tab_ref[d, 7] = pw_im[p0:p0 + SUBLANES]
    kt_ref[CHUNK - 1] = k_same.astype(kt_ref.dtype)


def _tile_params(x):
    x = x.reshape(2, N_LANE_TILES, 1, STATE_LANES)
    return jnp.transpose(x, (1, 0, 2, 3))


def s5_prepare(a_re, a_im, log_dt, b_re, b_im, c_re, c_im):
    g, p, gc = SSM_GROUPS, SSM_STATE, SSM_GROUP
    prow = jnp.stack([_tile_params(a_re), _tile_params(a_im),
                      _tile_params(jnp.broadcast_to(log_dt[:, :, None], (2, g, p)))], axis=2)

    def b_tiles(b):
        b = b.reshape(2, N_LANE_TILES, GROUPS_PER_TILE, p, gc)
        b = jnp.transpose(b, (1, 0, 4, 2, 3)).reshape(N_LANE_TILES, 2, 1, gc, STATE_LANES)
        return jnp.broadcast_to(b, (N_LANE_TILES, 2, GROUPS_PER_TILE, gc, STATE_LANES)).reshape(
            N_LANE_TILES, 2, LANES, STATE_LANES)

    def c_tiles(c):
        c = c.reshape(2, N_LANE_TILES, GROUPS_PER_TILE, gc, p)
        c = jnp.transpose(c, (1, 0, 2, 4, 3)).reshape(N_LANE_TILES, 2, STATE_LANES, 1, gc)
        return jnp.broadcast_to(c, (N_LANE_TILES, 2, STATE_LANES, GROUPS_PER_TILE, gc)).reshape(
            N_LANE_TILES, 2, STATE_LANES, LANES)

    b_t = jnp.stack([b_tiles(b_re), b_tiles(b_im)], axis=2)
    c_t = jnp.stack([c_tiles(c_re), c_tiles(c_im)], axis=2)
    expo = jnp.asarray(_pw_exponents())

    return pl.pallas_call(
        _s5_prep_kernel,
        grid=(N_LANE_TILES,),
        in_specs=[
            pl.BlockSpec((PW_ROWS, 1), lambda j: (0, 0)),
            pl.BlockSpec((None, 2, 3, 1, STATE_LANES), lambda j: (j, 0, 0, 0, 0)),
            pl.BlockSpec((None, 2, 2, LANES, STATE_LANES), lambda j: (j, 0, 0, 0, 0)),
            pl.BlockSpec((None, 2, 2, STATE_LANES, LANES), lambda j: (j, 0, 0, 0, 0)),
        ],
        out_specs=[
            pl.BlockSpec((None, CW, 4 * STATE_LANES), lambda j: (j, 0, 0)),
            pl.BlockSpec((None, 4 * STATE_LANES, CW), lambda j: (j, 0, 0)),
            pl.BlockSpec((None, N_LAGS, LANES, LANES), lambda j: (j, 0, 0, 0)),
            pl.BlockSpec((None, 2, 8, SUBLANES, STATE_LANES), lambda j: (j, 0, 0, 0, 0)),
        ],
        out_shape=[
            jax.ShapeDtypeStruct((N_LANE_TILES, CW, 4 * STATE_LANES), BF16),
            jax.ShapeDtypeStruct((N_LANE_TILES, 4 * STATE_LANES, CW), BF16),
            jax.ShapeDtypeStruct((N_LANE_TILES, N_LAGS, LANES, LANES), BF16),
            jax.ShapeDtypeStruct((N_LANE_TILES, 2, 8, SUBLANES, STATE_LANES), F32),
        ],
        compiler_params=_params(("arbitrary",)),
        name="s5_prepare",
    )(expo, prow, b_t, c_t)


def _cmul_add(xr, xi, ar, ai, yr, yi):
    return xr + ar * yr - ai * yi, xi + ar * yi + ai * yr


def _scan_sequence(x_ref, tab_ref, row0, n_rows):
    sl = STATE_LANES
    n_blk = n_rows // SUBLANES
    sub = lax.broadcasted_iota(jnp.int32, (SUBLANES, sl), 0)

    def one_block(d, row, carry):
        c0 = 2 * d * sl
        xr = x_ref[pl.ds(row, SUBLANES), c0:c0 + sl]
        xi = x_ref[pl.ds(row, SUBLANES), c0 + sl:c0 + 2 * sl]
        for k, sh in enumerate((1, 2, 4)):
            shift = sh if d == 0 else SUBLANES - sh
            yr = pltpu.roll(xr, shift, 0)
            yi = pltpu.roll(xi, shift, 0)
            xr, xi = _cmul_add(xr, xi, tab_ref[d, 2 * k], tab_ref[d, 2 * k + 1], yr, yi)
        cr = jnp.broadcast_to(carry[0], (SUBLANES, sl))
        ci = jnp.broadcast_to(carry[1], (SUBLANES, sl))
        xr, xi = _cmul_add(xr, xi, tab_ref[d, 6], tab_ref[d, 7], cr, ci)
        shift, vacated = (1, 0) if d == 0 else (SUBLANES - 1, SUBLANES - 1)
        x_ref[pl.ds(row, SUBLANES), c0:c0 + sl] = jnp.where(
            sub == vacated, cr, pltpu.roll(xr, shift, 0))
        x_ref[pl.ds(row, SUBLANES), c0 + sl:c0 + 2 * sl] = jnp.where(
            sub == vacated, ci, pltpu.roll(xi, shift, 0))
        edge = SUBLANES - 1 if d == 0 else 0
        return xr[edge:edge + 1], xi[edge:edge + 1]

    def body(i, carry):
        f_row = pl.multiple_of(row0 + i * SUBLANES, SUBLANES)
        b_row = pl.multiple_of(row0 + (n_blk - 1 - i) * SUBLANES, SUBLANES)
        cf = one_block(0, f_row, carry[0:2])
        cb = one_block(1, b_row, carry[2:4])
        return cf + cb

    z = jnp.zeros((1, sl), F32)
    lax.fori_loop(0, n_blk, body, (z, z, z, z))


def _s5_core_kernel(u_ref, bd_ref, cx_ref, kt_ref, tab_ref, d_ref, o_ref,
                    lhs_ref, x_ref, m_ref, stage_ref, *, bk, seq_chunks):
    phase = pl.program_id(1)
    cb = pl.program_id(2)
    n_cb = pl.num_programs(2)
    off = pl.multiple_of(cb * bk, bk)
    starts = [sum(seq_chunks[:k]) for k in range(len(seq_chunks))]

    u_slots = [u_ref[pl.ds(s, bk, stride=CHUNK), :] for s in range(CHUNK)]
    for s in range(CHUNK):
        lhs_ref[:, s * LANES:(s + 1) * LANES] = u_slots[s].astype(BF16)

    @pl.when(phase == 0)
    def _():
        x_ref[pl.ds(off, bk), :] = _dot(lhs_ref[...], bd_ref[...])

        @pl.when(cb == n_cb - 1)
        def _():
            for st, n in zip(starts, seq_chunks):
                _scan_sequence(x_ref, tab_ref, st, n)

    @pl.when(phase == 1)
    def _():
        @pl.when(cb == 0)
        def _():
            for s in range(CHUNK):
                for t in range(CHUNK):
                    m_ref[s * LANES:(s + 1) * LANES, t * LANES:(t + 1) * LANES] = (
                        kt_ref[CHUNK - 1 + t - s])

        y = _dot(lhs_ref[...], m_ref[...])
        y = y + _dot(x_ref[pl.ds(off, bk), :].astype(BF16), cx_ref[...])
        dvec = d_ref[...]
        for t in range(CHUNK):
            yt = y[:, t * LANES:(t + 1) * LANES] + dvec * u_slots[t]
            stage_ref[pl.ds(t, bk, stride=CHUNK), :] = jax.nn.gelu(yt)
        o_ref[...] = stage_ref[...].astype(o_ref.dtype)


def s5_core(u, bd, cx, kt, tab, d, *, seq_chunks, bk=128):
    m = u.shape[0]
    bk = math.gcd(bk, *seq_chunks)
    n_chunks = m // CHUNK
    assert n_chunks == sum(seq_chunks)
    kern = functools.partial(_s5_core_kernel, bk=bk, seq_chunks=tuple(seq_chunks))
    single = pl.Buffered(1)
    return pl.pallas_call(
        kern,
        grid=(N_LANE_TILES, 2, n_chunks // bk),
        in_specs=[
            pl.BlockSpec((bk * CHUNK, LANES), lambda j, ph, c: (c, j)),
            pl.BlockSpec((None, CW, 4 * STATE_LANES), lambda j, ph, c: (j, 0, 0),
                         pipeline_mode=single),
            pl.BlockSpec((None, 4 * STATE_LANES, CW), lambda j, ph, c: (j, 0, 0),
                         pipeline_mode=single),
            pl.BlockSpec((None, N_LAGS, LANES, LANES), lambda j, ph, c: (j, 0, 0, 0)),
            pl.BlockSpec((None, 2, 8, SUBLANES, STATE_LANES), lambda j, ph, c: (j, 0, 0, 0, 0)),
            pl.BlockSpec((1, LANES), lambda j, ph, c: (0, j)),
        ],
        out_specs=pl.BlockSpec((bk * CHUNK, LANES), lambda j, ph, c: (c * ph, j)),
        out_shape=jax.ShapeDtypeStruct((m, D_MODEL), BF16),
        scratch_shapes=[
            pltpu.VMEM((bk, CW), BF16),
            pltpu.VMEM((n_chunks, 4 * STATE_LANES), F32),
            pltpu.VMEM((CW, CW), BF16),
            pltpu.VMEM((bk * CHUNK, LANES), F32),
        ],
        compiler_params=_params(("arbitrary", "arbitrary", "arbitrary")),
        name="s5_core",
    )(u, bd, cx, kt, tab, d.reshape(1, D_MODEL))


def _trunk(h, p_all, seq_lens, norm_mix, norm_ffn, norm_ple, s5_w_in, s5_a_re, s5_a_im,
           s5_log_dt, s5_b_re, s5_b_im, s5_c_re, s5_c_im, s5_d, s5_w_glu, attn_w_qkv,
           attn_q_norm, attn_k_norm, attn_rpb, attn_w_o, ffn_w_gate, ffn_w_up, ffn_w_down,
           ple_w_gate, ple_w_proj):
    seq_chunks = tuple(n // CHUNK for n in seq_lens)
    for i in range(DEPTH):
        j = i // 2
        if i % 2 == 0:
            bd, cx, kt, tab = s5_prepare(s5_a_re[j], s5_a_im[j], s5_log_dt[j], s5_b_re[j],
                                         s5_b_im[j], s5_c_re[j], s5_c_im[j])
            u = s5_in_proj(h, norm_mix[i], s5_w_in[j].astype(BF16))
            g = s5_core(u, bd, cx, kt, tab, s5_d[j], seq_chunks=seq_chunks)
            h = s5_glu_residual(g, s5_w_glu[j].astype(BF16), h)
        else:
            qkv = qkv_proj(h, norm_mix[i], attn_w_qkv[j].astype(BF16), attn_q_norm[j],
                           attn_k_norm[j])
            att, row0 = None, 0
            for n in seq_lens:
                assert row0 % n == 0
                att = neighborhood_attention(qkv, attn_rpb[j], att, seq_len=n,
                                             row_block=row0 // n)
                row0 += n
            h = attn_out_residual(att, attn_w_o[j].astype(BF16), h)
        h = ffn_residual(h, norm_ffn[i], ffn_w_gate[i].astype(BF16), ffn_w_up[i].astype(BF16),
                         ffn_w_down[i].astype(BF16))
        h = ple_residual(h, norm_ple[i], ple_w_gate[i].astype(BF16), p_all[i],
                         ple_w_proj[i].astype(BF16))
    return h


def kernel(x_prompt, x_sample, p_prompt, p_sample, norm_mix, norm_ffn, norm_ple, s5_w_in, s5_a_re, s5_a_im, s5_log_dt, s5_b_re, s5_b_im, s5_c_re, s5_c_im, s5_d, s5_w_glu, attn_w_qkv, attn_q_norm, attn_k_norm, attn_rpb, attn_w_o, ffn_w_gate, ffn_w_up, ffn_w_down, ple_w_gate, ple_w_proj):
    assert x_prompt.shape[0] == 1 and x_sample.shape[0] == 1
    n_s, n_p = x_sample.shape[1], x_prompt.shape[1]
    h = jnp.concatenate([x_sample[0], x_prompt[0]], axis=0)
    p_all = jnp.concatenate([p_sample[:, 0], p_prompt[:, 0]], axis=1).astype(BF16)
    h = _trunk(h, p_all, (n_s, n_p), norm_mix, norm_ffn, norm_ple, s5_w_in, s5_a_re, s5_a_im,
               s5_log_dt, s5_b_re, s5_b_im, s5_c_re, s5_c_im, s5_d, s5_w_glu, attn_w_qkv,
               attn_q_norm, attn_k_norm, attn_rpb, attn_w_o, ffn_w_gate, ffn_w_up, ffn_w_down,
               ple_w_gate, ple_w_proj)
    return h[n_s:][None], h[:n_s][None]
```

```python
import functools
import math

import numpy as np
import jax
import jax.numpy as jnp
from jax import lax
from jax.experimental import pallas as pl
from jax.experimental.pallas import tpu as pltpu

F32 = jnp.float32
BF16 = jnp.bfloat16

D_MODEL = 2048
DEPTH = 4
PLE_DIM = 256
GRID_W = 64
SSM_GROUP = 16
SSM_GROUPS = D_MODEL // SSM_GROUP
SSM_STATE = 64
HEAD_DIM = 128
N_HEADS = D_MODEL // HEAD_DIM
WIN_H = 8
WIN_W = 16
D_FF = 5632
EPS = 1e-6

LANES = 128
SUBLANES = 8
VMEM_LIMIT_BYTES = 56 * 1024 * 1024

TN = 512
CHUNK = 16
GROUPS_PER_TILE = LANES // SSM_GROUP
N_LANE_TILES = D_MODEL // LANES
STATE_LANES = GROUPS_PER_TILE * SSM_STATE
CW = CHUNK * LANES
N_LAGS = 2 * CHUNK - 1
PW_ROWS = 128
NEG_BIG = -1e30


def _params(sem, vmem=VMEM_LIMIT_BYTES):
    return pltpu.CompilerParams(dimension_semantics=sem, vmem_limit_bytes=vmem)


def _rmsnorm_bf16(x, gain):
    ms = jnp.mean(x * x, axis=-1, keepdims=True)
    return (x * lax.rsqrt(ms + EPS) * gain).astype(BF16)


def _dot(a, b):
    return jnp.dot(a, b, preferred_element_type=F32)


def _resident(shape):
    return pl.BlockSpec(shape, lambda *_: (0,) * len(shape), pipeline_mode=pl.Buffered(1))


def _col_chunks(width):
    return [slice(n * TN, (n + 1) * TN) for n in range(width // TN)]


def _in_proj_kernel(x_ref, g_ref, w_ref, o_ref, xn_ref):
    xn_ref[...] = _rmsnorm_bf16(x_ref[...], g_ref[...])
    for cols in _col_chunks(D_MODEL):
        o_ref[:, cols] = _dot(xn_ref[...], w_ref[:, cols])


def s5_in_proj(h, gain, w, *, tm=512):
    m = h.shape[0]
    tm = math.gcd(tm, m)
    return pl.pallas_call(
        _in_proj_kernel,
        grid=(m // tm,),
        in_specs=[
            pl.BlockSpec((tm, D_MODEL), lambda i: (i, 0)),
            _resident((1, D_MODEL)),
            _resident((D_MODEL, D_MODEL)),
        ],
        out_specs=pl.BlockSpec((tm, D_MODEL), lambda i: (i, 0)),
        out_shape=jax.ShapeDtypeStruct((m, D_MODEL), F32),
        scratch_shapes=[pltpu.VMEM((tm, D_MODEL), BF16)],
        compiler_params=_params(("parallel",)),
        name="s5_in_proj",
    )(h, gain.reshape(1, D_MODEL), w)


def _qkv_kernel(x_ref, g_ref, w_ref, qg_ref, kg_ref, o_ref, xn_ref):
    xn_ref[...] = _rmsnorm_bf16(x_ref[...], g_ref[...])
    q_gain = qg_ref[...] * HEAD_DIM ** -0.5
    k_gain = kg_ref[...]

    def head_norm(acc, gain):
        outs = []
        for hh in range(TN // HEAD_DIM):
            t = acc[:, hh * HEAD_DIM:(hh + 1) * HEAD_DIM]
            ms = jnp.mean(t * t, axis=-1, keepdims=True)
            outs.append(t * lax.rsqrt(ms + EPS) * gain)
        return jnp.concatenate(outs, axis=-1)

    for cols in _col_chunks(3 * D_MODEL):
        acc = _dot(xn_ref[...], w_ref[:, cols])
        if cols.start < D_MODEL:
            acc = head_norm(acc, q_gain)
        elif cols.start < 2 * D_MODEL:
            acc = head_norm(acc, k_gain)
        o_ref[:, cols] = acc.astype(o_ref.dtype)


def qkv_proj(h, gain, w, q_gain, k_gain, *, tm=512):
    m = h.shape[0]
    n_out = w.shape[1]
    tm = math.gcd(tm, m)
    return pl.pallas_call(
        _qkv_kernel,
        grid=(m // tm,),
        in_specs=[
            pl.BlockSpec((tm, D_MODEL), lambda i: (i, 0)),
            _resident((1, D_MODEL)),
            _resident((D_MODEL, n_out)),
            _resident((1, HEAD_DIM)),
            _resident((1, HEAD_DIM)),
        ],
        out_specs=pl.BlockSpec((tm, n_out), lambda i: (i, 0)),
        out_shape=jax.ShapeDtypeStruct((m, n_out), BF16),
        scratch_shapes=[pltpu.VMEM((tm, D_MODEL), BF16)],
        compiler_params=_params(("parallel",)),
        name="qkv_proj",
    )(h, gain.reshape(1, D_MODEL), w, q_gain.reshape(1, HEAD_DIM), k_gain.reshape(1, HEAD_DIM))


def _ffn_kernel(h_ref, g_ref, wg_ref, wu_ref, wd_ref, o_ref, xn_ref):
    @pl.when(pl.program_id(1) == 0)
    def _():
        x = h_ref[...]
        xn_ref[...] = _rmsnorm_bf16(x, g_ref[...])
        o_ref[...] = x

    xn = xn_ref[...]
    gate = _dot(xn, wg_ref[...])
    up = _dot(xn, wu_ref[...])
    act = (jax.nn.silu(gate) * up).astype(BF16)
    o_ref[...] += _dot(act, wd_ref[...])


def ffn_residual(h, gain, w_gate, w_up, w_down, *, tm=768, tf=512):
    m = h.shape[0]
    tm = math.gcd(tm, m)
    return pl.pallas_call(
        _ffn_kernel,
        grid=(m // tm, D_FF // tf),
        in_specs=[
            pl.BlockSpec((tm, D_MODEL), lambda i, f: (i, 0)),
            pl.BlockSpec((1, D_MODEL), lambda i, f: (0, 0)),
            pl.BlockSpec((D_MODEL, tf), lambda i, f: (0, f)),
            pl.BlockSpec((D_MODEL, tf), lambda i, f: (0, f)),
            pl.BlockSpec((tf, D_MODEL), lambda i, f: (f, 0)),
        ],
        out_specs=pl.BlockSpec((tm, D_MODEL), lambda i, f: (i, 0)),
        out_shape=jax.ShapeDtypeStruct((m, D_MODEL), F32),
        scratch_shapes=[pltpu.VMEM((tm, D_MODEL), BF16)],
        compiler_params=_params(("parallel", "arbitrary")),
        name="ffn_residual",
    )(h, gain.reshape(1, D_MODEL), w_gate, w_up, w_down)


def _ple_kernel(x_ref, g_ref, wg_ref, p_ref, wp_ref, *rest, first_tiles):
    o_refs, xn_ref = rest[:-1], rest[-1]
    xn_ref[...] = _rmsnorm_bf16(x_ref[...], g_ref[...])

    def emit(o_ref):
        for cols in _col_chunks(D_MODEL):
            z = _dot(xn_ref[...], wg_ref[:, cols])
            proj = _dot(p_ref[...], wp_ref[:, cols])
            o_ref[:, cols] = x_ref[:, cols] + jax.nn.sigmoid(z) * proj

    if len(o_refs) == 1:
        emit(o_refs[0])
    else:
        pl.when(pl.program_id(0) < first_tiles)(lambda: emit(o_refs[0]))
        pl.when(pl.program_id(0) >= first_tiles)(lambda: emit(o_refs[1]))


def ple_residual(h, gain, w_gate, p, w_proj, *, split_rows=None, tm=512):
    m = h.shape[0]
    tm = math.gcd(tm, m if split_rows is None else math.gcd(m, split_rows))
    if split_rows is None:
        first_tiles = m // tm
        out_specs = pl.BlockSpec((tm, D_MODEL), lambda i: (i, 0))
        out_shape = jax.ShapeDtypeStruct((m, D_MODEL), F32)
    else:
        first_tiles = split_rows // tm
        out_specs = [
            pl.BlockSpec((tm, D_MODEL), lambda i: (jnp.minimum(i, first_tiles - 1), 0)),
            pl.BlockSpec((tm, D_MODEL), lambda i: (jnp.maximum(i - first_tiles, 0), 0)),
        ]
        out_shape = [jax.ShapeDtypeStruct((split_rows, D_MODEL), F32),
                     jax.ShapeDtypeStruct((m - split_rows, D_MODEL), F32)]
    return pl.pallas_call(
        functools.partial(_ple_kernel, first_tiles=first_tiles),
        grid=(m // tm,),
        in_specs=[
            pl.BlockSpec((tm, D_MODEL), lambda i: (i, 0)),
            _resident((1, D_MODEL)),
            _resident((D_MODEL, D_MODEL)),
            pl.BlockSpec((tm, PLE_DIM), lambda i: (i, 0)),
            _resident((PLE_DIM, D_MODEL)),
        ],
        out_specs=out_specs,
        out_shape=out_shape,
        scratch_shapes=[pltpu.VMEM((tm, D_MODEL), BF16)],
        compiler_params=_params(("arbitrary",)),
        name="ple_residual",
    )(h, gain.reshape(1, D_MODEL), w_gate, p, w_proj)


def _glu_kernel(g_ref, w_ref, h_ref, o_ref):
    for n, cols in enumerate(_col_chunks(D_MODEL)):
        gate_cols = slice(D_MODEL + n * TN, D_MODEL + (n + 1) * TN)
        a = _dot(g_ref[...], w_ref[:, cols])
        b = _dot(g_ref[...], w_ref[:, gate_cols])
        o_ref[:, cols] = h_ref[:, cols] + a * jax.nn.sigmoid(b)


def s5_glu_residual(g, w_glu, h, *, tm=512):
    m = h.shape[0]
    tm = math.gcd(tm, m)
    return pl.pallas_call(
        _glu_kernel,
        grid=(m // tm,),
        in_specs=[
            pl.BlockSpec((tm, D_MODEL), lambda i: (i, 0)),
            _resident((D_MODEL, 2 * D_MODEL)),
            pl.BlockSpec((tm, D_MODEL), lambda i: (i, 0)),
        ],
        out_specs=pl.BlockSpec((tm, D_MODEL), lambda i: (i, 0)),
        out_shape=jax.ShapeDtypeStruct((m, D_MODEL), F32),
        compiler_params=_params(("parallel",)),
        name="s5_glu_residual",
    )(g, w_glu, h)


def _proj_residual_kernel(a_ref, w_ref, h_ref, o_ref):
    for cols in _col_chunks(D_MODEL):
        o_ref[:, cols] = h_ref[:, cols] + _dot(a_ref[...], w_ref[:, cols])


def attn_out_residual(a, w_o, h, *, tm=512):
    m = h.shape[0]
    tm = math.gcd(tm, m)
    return pl.pallas_call(
        _proj_residual_kernel,
        grid=(m // tm,),
        in_specs=[
            pl.BlockSpec((tm, D_MODEL), lambda i: (i, 0)),
            _resident((D_MODEL, D_MODEL)),
            pl.BlockSpec((tm, D_MODEL), lambda i: (i, 0)),
        ],
        out_specs=pl.BlockSpec((tm, D_MODEL), lambda i: (i, 0)),
        out_shape=jax.ShapeDtypeStruct((m, D_MODEL), F32),
        compiler_params=_params(("parallel",)),
        name="attn_out_residual",
    )(a, w_o, h)


def _attn_plan(rows, r_blk):
    span = r_blk + WIN_H - 1
    variants, vid = [], []
    for i in range(rows // r_blk):
        base = min(max(r_blk * i - WIN_H // 2, 0), rows - span)
        key = tuple(min(max(r - WIN_H // 2, 0), rows - WIN_H) - base
                    for r in range(r_blk * i, r_blk * (i + 1))) + (r_blk * i - base,)
        if key not in variants:
            variants.append(key)
        vid.append(variants.index(key))
    return span, variants, np.asarray(vid, np.int32)


def _attn_bias_tables(rpb, rows, r_blk):
    span, variants, vid = _attn_plan(rows, r_blk)
    cols = np.arange(GRID_W)
    col_start = np.clip(cols - WIN_W // 2, 0, GRID_W - WIN_W)
    in_win = (cols[None, :] >= col_start[:, None]) & (cols[None, :] < col_start[:, None] + WIN_W)
    pad = GRID_W - WIN_W
    rp = jnp.pad(rpb.astype(F32), ((0, 0), (0, 0), (pad, pad)))
    band = jnp.stack([rp[:, :, GRID_W - 1 - c:2 * GRID_W - 1 - c] for c in range(GRID_W)], axis=2)
    band = jnp.where(in_win[None, None], band, NEG_BIG)
    outside = jnp.full((N_HEADS, GRID_W, GRID_W), NEG_BIG, F32)
    tables = []
    for key in variants:
        q0 = key[-1]
        row_blocks = []
        for rl in range(r_blk):
            rs = key[rl]
            pieces = [band[:, kr - (q0 + rl) + WIN_H - 1] if rs <= kr < rs + WIN_H else outside
                      for kr in range(span)]
            row_blocks.append(jnp.concatenate(pieces, axis=-1))
        tables.append(jnp.concatenate(row_blocks, axis=1))
    return jnp.stack(tables, axis=1), vid


def _attn_kernel(vid_ref, q_ref, k_ref, v_ref, b_ref, *rest, rows, r_blk, unroll):
    o_ref = rest[-1]
    span = r_blk + WIN_H - 1
    nq, nk = r_blk * GRID_W, span * GRID_W

    def body(i, carry):
        base = jnp.clip(r_blk * i - WIN_H // 2, 0, rows - span)
        q0 = pl.multiple_of(i * nq, GRID_W)
        k0 = pl.multiple_of(base * GRID_W, GRID_W)
        q = q_ref[pl.ds(q0, nq), :]
        kb = k_ref[pl.ds(k0, nk), :]
        vb = v_ref[pl.ds(k0, nk), :]
        s = lax.dot_general(q, kb, (((1,), (1,)), ((), ())), preferred_element_type=F32)
        s = s + b_ref[vid_ref[i]]
        m = jnp.max(s, axis=-1, keepdims=True)
        e = jnp.exp(s - m)
        denom = jnp.sum(e, axis=-1, keepdims=True)
        o = _dot(e.astype(BF16), vb) / denom
        o_ref[pl.ds(q0, nq), :] = o.astype(o_ref.dtype)
        return carry

    lax.fori_loop(0, rows // r_blk, body, 0, unroll=unroll)


def neighborhood_attention(qkv, rpb, prev_out, *, seq_len, row_block, r_blk=8, unroll=2):
    rows = seq_len // GRID_W
    bias, vid = _attn_bias_tables(rpb, rows, r_blk)
    n_var, nq, nk = bias.shape[1:]
    n_blk = rows // r_blk
    kern = functools.partial(_attn_kernel, rows=rows, r_blk=r_blk,
                             unroll=math.gcd(unroll, n_blk))
    in_specs = [
        pl.BlockSpec((seq_len, HEAD_DIM), lambda h, vid: (row_block, h)),
        pl.BlockSpec((seq_len, HEAD_DIM), lambda h, vid: (row_block, N_HEADS + h)),
        pl.BlockSpec((seq_len, HEAD_DIM), lambda h, vid: (row_block, 2 * N_HEADS + h)),
        pl.BlockSpec((None, n_var, nq, nk), lambda h, vid: (h, 0, 0, 0)),
    ]
    args = [jnp.asarray(vid), qkv, qkv, qkv, bias]
    aliases = {}
    if prev_out is not None:
        in_specs.append(pl.BlockSpec(memory_space=pl.ANY))
        args.append(prev_out)
        aliases = {len(args) - 1: 0}
    grid_spec = pltpu.PrefetchScalarGridSpec(
        num_scalar_prefetch=1,
        grid=(N_HEADS,),
        in_specs=in_specs,
        out_specs=pl.BlockSpec((seq_len, HEAD_DIM), lambda h, vid: (row_block, h)),
    )
    return pl.pallas_call(
        kern,
        grid_spec=grid_spec,
        out_shape=jax.ShapeDtypeStruct((qkv.shape[0], D_MODEL), BF16),
        input_output_aliases=aliases,
        compiler_params=_params(("arbitrary",)),
        name="neighborhood_attention",
    )(*args)


def _pw_exponents():
    e = np.zeros((PW_ROWS, 1), np.float32)
    for r in range(CHUNK + 1):
        e[r, 0] = r
    for r in range(CHUNK + 1, CHUNK + SUBLANES):
        e[r, 0] = CHUNK * (r - CHUNK + 1)
    for r in range(CHUNK + SUBLANES, CHUNK + 2 * SUBLANES):
        e[r, 0] = CHUNK * (CHUNK + 2 * SUBLANES - r)
    return e


def _s5_prep_kernel(e_ref, prow_ref, b_ref, c_ref, bd_ref, cx_ref, kt_ref, tab_ref, blk_ref):
    expo = e_ref[...]
    row_g = lax.broadcasted_iota(jnp.int32, (LANES, STATE_LANES), 0) // SSM_GROUP
    col_g = lax.broadcasted_iota(jnp.int32, (LANES, STATE_LANES), 1) // SSM_STATE
    b_mask = row_g == col_g
    row_gc = lax.broadcasted_iota(jnp.int32, (STATE_LANES, LANES), 0) // SSM_STATE
    col_gc = lax.broadcasted_iota(jnp.int32, (STATE_LANES, LANES), 1) // SSM_GROUP
    c_mask = row_gc == col_gc
    sub = lax.broadcasted_iota(jnp.int32, (SUBLANES, STATE_LANES), 0)

    k_same = None
    for d in range(2):
        a_re = prow_ref[d, 0]
        a_im = prow_ref[d, 1]
        dt = jnp.exp(prow_ref[d, 2])
        mag = jnp.exp(expo * (a_re * dt))
        ang = expo * (a_im * dt)
        pw_re = mag * jnp.cos(ang)
        pw_im = mag * jnp.sin(ang)
        pw_re_t = pw_re.T
        pw_im_t = pw_im.T

        xr = pw_re[1:2] - 1.0
        xi = pw_im[1:2]
        den = a_re * a_re + a_im * a_im
        f_re = (xr * a_re + xi * a_im) / den
        f_im = (xi * a_re - xr * a_im) / den
        b_re = jnp.where(b_mask, b_ref[d, 0], 0.0)
        b_im = jnp.where(b_mask, b_ref[d, 1], 0.0)
        bb_re = f_re * b_re - f_im * b_im
        bb_im = f_re * b_im + f_im * b_re
        c_re = jnp.where(c_mask, c_ref[d, 0], 0.0)
        c_im = jnp.where(c_mask, c_ref[d, 1], 0.0)
        c_cat = jnp.concatenate([c_re, -c_im], axis=0)

        col0 = 2 * d * STATE_LANES
        for s in range(CHUNK):
            n_in = CHUNK - 1 - s if d == 0 else s
            pr, pi = pw_re[n_in:n_in + 1], pw_im[n_in:n_in + 1]
            blk_re = bb_re * pr - bb_im * pi
            blk_im = bb_re * pi + bb_im * pr
            rows = slice(s * LANES, (s + 1) * LANES)
            bd_ref[rows, col0:col0 + STATE_LANES] = blk_re.astype(bd_ref.dtype)
            bd_ref[rows, col0 + STATE_LANES:col0 + 2 * STATE_LANES] = blk_im.astype(bd_ref.dtype)
            blk_ref[rows, 0:STATE_LANES] = blk_re
            blk_ref[rows, STATE_LANES:2 * STATE_LANES] = blk_im

            n_out = s + 1 if d == 0 else CHUNK - s
            qr = pw_re_t[:, n_out:n_out + 1]
            qi = pw_im_t[:, n_out:n_out + 1]
            cols = slice(s * LANES, (s + 1) * LANES)
            cx_ref[col0:col0 + STATE_LANES, cols] = (c_re * qr - c_im * qi).astype(cx_ref.dtype)
            cx_ref[col0 + STATE_LANES:col0 + 2 * STATE_LANES, cols] = (
                -(c_re * qi + c_im * qr)).astype(cx_ref.dtype)

        k_all = jnp.dot(blk_ref[...], c_cat, precision=lax.Precision.HIGHEST,
                        preferred_element_type=F32)
        for s in range(CHUNK):
            n_in = CHUNK - 1 - s if d == 0 else s
            k_lag = k_all[s * LANES:(s + 1) * LANES]
            if n_in == 0:
                k_same = k_lag if k_same is None else k_same + k_lag
            else:
                lag = n_in if d == 0 else -n_in
                kt_ref[CHUNK - 1 + lag] = k_lag.astype(kt_ref.dtype)

        for k, sh in enumerate((1, 2, 4)):
            r = CHUNK + sh - 1 if sh < 4 else CHUNK + 3
            keep = (sub >= sh) if d == 0 else (sub < SUBLANES - sh)
            tab_ref[d, 2 * k] = jnp.where(keep, pw_re[r:r + 1], 0.0)
            tab_ref[d, 2 * k + 1] = jnp.where(keep, pw_im[r:r + 1], 0.0)
        p0 = CHUNK if d == 0 else CHUNK + SUBLANES
        tab_ref[d, 6] = pw_re[p0:p0 + SUBLANES]
        tab_ref[d, 7] = pw_im[p0:p0 + SUBLANES]
    kt_ref[CHUNK - 1] = k_same.astype(kt_ref.dtype)


def _tile_params(x):
    x = x.reshape(2, N_LANE_TILES, 1, STATE_LANES)
    return jnp.transpose(x, (1, 0, 2, 3))


def s5_prepare(a_re, a_im, log_dt, b_re, b_im, c_re, c_im):
    g, p, gc = SSM_GROUPS, SSM_STATE, SSM_GROUP
    prow = jnp.stack([_tile_params(a_re), _tile_params(a_im),
                      _tile_params(jnp.broadcast_to(log_dt[:, :, None], (2, g, p)))], axis=2)

    def b_tiles(b):
        b = b.reshape(2, N_LANE_TILES, GROUPS_PER_TILE, p, gc)
        b = jnp.transpose(b, (1, 0, 4, 2, 3)).reshape(N_LANE_TILES, 2, 1, gc, STATE_LANES)
        return jnp.broadcast_to(b, (N_LANE_TILES, 2, GROUPS_PER_TILE, gc, STATE_LANES)).reshape(
            N_LANE_TILES, 2, LANES, STATE_LANES)

    def c_tiles(c):
        c = c.reshape(2, N_LANE_TILES, GROUPS_PER_TILE, gc, p)
        c = jnp.transpose(c, (1, 0, 2, 4, 3)).reshape(N_LANE_TILES, 2, STATE_LANES, 1, gc)
        return jnp.broadcast_to(c, (N_LANE_TILES, 2, STATE_LANES, GROUPS_PER_TILE, gc)).reshape(
            N_LANE_TILES, 2, STATE_LANES, LANES)

    b_t = jnp.stack([b_tiles(b_re), b_tiles(b_im)], axis=2)
    c_t = jnp.stack([c_tiles(c_re), c_tiles(c_im)], axis=2)
    expo = jnp.asarray(_pw_exponents())

    return pl.pallas_call(
        _s5_prep_kernel,
        grid=(N_LANE_TILES,),
        in_specs=[
            pl.BlockSpec((PW_ROWS, 1), lambda j: (0, 0)),
            pl.BlockSpec((None, 2, 3, 1, STATE_LANES), lambda j: (j, 0, 0, 0, 0)),
            pl.BlockSpec((None, 2, 2, LANES, STATE_LANES), lambda j: (j, 0, 0, 0, 0)),
            pl.BlockSpec((None, 2, 2, STATE_LANES, LANES), lambda j: (j, 0, 0, 0, 0)),
        ],
        out_specs=[
            pl.BlockSpec((None, CW, 4 * STATE_LANES), lambda j: (j, 0, 0)),
            pl.BlockSpec((None, 4 * STATE_LANES, CW), lambda j: (j, 0, 0)),
            pl.BlockSpec((None, N_LAGS, LANES, LANES), lambda j: (j, 0, 0, 0)),
            pl.BlockSpec((None, 2, 8, SUBLANES, STATE_LANES), lambda j: (j, 0, 0, 0, 0)),
        ],
        out_shape=[
            jax.ShapeDtypeStruct((N_LANE_TILES, CW, 4 * STATE_LANES), BF16),
            jax.ShapeDtypeStruct((N_LANE_TILES, 4 * STATE_LANES, CW), BF16),
            jax.ShapeDtypeStruct((N_LANE_TILES, N_LAGS, LANES, LANES), BF16),
            jax.ShapeDtypeStruct((N_LANE_TILES, 2, 8, SUBLANES, STATE_LANES), F32),
        ],
        scratch_shapes=[pltpu.VMEM((CW, 2 * STATE_LANES), F32)],
        compiler_params=_params(("arbitrary",)),
        name="s5_prepare",
    )(expo, prow, b_t, c_t)


def _cmul_add(xr, xi, ar, ai, yr, yi):
    return xr + ar * yr - ai * yi, xi + ar * yi + ai * yr


def _scan_sequence(x_ref, tab_ref, row0, n_rows):
    sl = STATE_LANES
    n_blk = n_rows // SUBLANES
    sub = lax.broadcasted_iota(jnp.int32, (SUBLANES, sl), 0)

    def one_block(d, row, carry):
        c0 = 2 * d * sl
        xr = x_ref[pl.ds(row, SUBLANES), c0:c0 + sl]
        xi = x_ref[pl.ds(row, SUBLANES), c0 + sl:c0 + 2 * sl]
        for k, sh in enumerate((1, 2, 4)):
            shift = sh if d == 0 else SUBLANES - sh
            yr = pltpu.roll(xr, shift, 0)
            yi = pltpu.roll(xi, shift, 0)
            xr, xi = _cmul_add(xr, xi, tab_ref[d, 2 * k], tab_ref[d, 2 * k + 1], yr, yi)
        cr = jnp.broadcast_to(carry[0], (SUBLANES, sl))
        ci = jnp.broadcast_to(carry[1], (SUBLANES, sl))
        xr, xi = _cmul_add(xr, xi, tab_ref[d, 6], tab_ref[d, 7], cr, ci)
        shift, vacated = (1, 0) if d == 0 else (SUBLANES - 1, SUBLANES - 1)
        x_ref[pl.ds(row, SUBLANES), c0:c0 + sl] = jnp.where(
            sub == vacated, cr, pltpu.roll(xr, shift, 0))
        x_ref[pl.ds(row, SUBLANES), c0 + sl:c0 + 2 * sl] = jnp.where(
            sub == vacated, ci, pltpu.roll(xi, shift, 0))
        edge = SUBLANES - 1 if d == 0 else 0
        return xr[edge:edge + 1], xi[edge:edge + 1]

    def body(i, carry):
        f_row = pl.multiple_of(row0 + i * SUBLANES, SUBLANES)
        b_row = pl.multiple_of(row0 + (n_blk - 1 - i) * SUBLANES, SUBLANES)
        cf = one_block(0, f_row, carry[0:2])
        cb = one_block(1, b_row, carry[2:4])
        return cf + cb

    z = jnp.zeros((1, sl), F32)
    lax.fori_loop(0, n_blk, body, (z, z, z, z))


def _s5_core_kernel(u_ref, bd_ref, cx_ref, kt_ref, tab_ref, d_ref, o_ref,
                    lhs_ref, x_ref, m_ref, stage_ref, *, bk, seq_chunks):
    phase = pl.program_id(1)
    cb = pl.program_id(2)
    n_cb = pl.num_programs(2)
    off = pl.multiple_of(cb * bk, bk)
    starts = [sum(seq_chunks[:k]) for k in range(len(seq_chunks))]

    u_slots = [u_ref[pl.ds(s, bk, stride=CHUNK), :] for s in range(CHUNK)]
    for s in range(CHUNK):
        lhs_ref[:, s * LANES:(s + 1) * LANES] = u_slots[s].astype(BF16)

    @pl.when(phase == 0)
    def _():
        x_ref[pl.ds(off, bk), :] = _dot(lhs_ref[...], bd_ref[...])

        @pl.when(cb == n_cb - 1)
        def _():
            for st, n in zip(starts, seq_chunks):
                _scan_sequence(x_ref, tab_ref, st, n)

    @pl.when(phase == 1)
    def _():
        @pl.when(cb == 0)
        def _():
            for s in range(CHUNK):
                for t in range(CHUNK):
                    m_ref[s * LANES:(s + 1) * LANES, t * LANES:(t + 1) * LANES] = (
                        kt_ref[CHUNK - 1 + t - s])

        y = _dot(lhs_ref[...], m_ref[...])
        y = y + _dot(x_ref[pl.ds(off, bk), :].astype(BF16), cx_ref[...])
        dvec = d_ref[...]
        for t in range(CHUNK):
            yt = y[:, t * LANES:(t + 1) * LANES] + dvec * u_slots[t]
            stage_ref[pl.ds(t, bk, stride=CHUNK), :] = jax.nn.gelu(yt)
        o_ref[...] = stage_ref[...].astype(o_ref.dtype)


def s5_core(u, bd, cx, kt, tab, d, *, seq_chunks, bk=128):
    m = u.shape[0]
    bk = math.gcd(bk, *seq_chunks)
    n_chunks = m // CHUNK
    assert n_chunks == sum(seq_chunks)
    kern = functools.partial(_s5_core_kernel, bk=bk, seq_chunks=tuple(seq_chunks))
    single = pl.Buffered(1)
    return pl.pallas_call(
        kern,
        grid=(N_LANE_TILES, 2, n_chunks // bk),
        in_specs=[
            pl.BlockSpec((bk * CHUNK, LANES), lambda j, ph, c: (c, j)),
            pl.BlockSpec((None, CW, 4 * STATE_LANES), lambda j, ph, c: (j, 0, 0),
                         pipeline_mode=single),
            pl.BlockSpec((None, 4 * STATE_LANES, CW), lambda j, ph, c: (j, 0, 0),
                         pipeline_mode=single),
            pl.BlockSpec((None, N_LAGS, LANES, LANES), lambda j, ph, c: (j, 0, 0, 0)),
            pl.BlockSpec((None, 2, 8, SUBLANES, STATE_LANES), lambda j, ph, c: (j, 0, 0, 0, 0)),
            pl.BlockSpec((1, LANES), lambda j, ph, c: (0, j)),
        ],
        out_specs=pl.BlockSpec((bk * CHUNK, LANES), lambda j, ph, c: (c * ph, j)),
        out_shape=jax.ShapeDtypeStruct((m, D_MODEL), BF16),
        scratch_shapes=[
            pltpu.VMEM((bk, CW), BF16),
            pltpu.VMEM((n_chunks, 4 * STATE_LANES), F32),
            pltpu.VMEM((CW, CW), BF16),
            pltpu.VMEM((bk * CHUNK, LANES), F32),
        ],
        compiler_params=_params(("arbitrary", "arbitrary", "arbitrary")),
        name="s5_core",
    )(u, bd, cx, kt, tab, d.reshape(1, D_MODEL))


def _trunk(h, p_all, seq_lens, norm_mix, norm_ffn, norm_ple, s5_w_in, s5_a_re, s5_a_im,
           s5_log_dt, s5_b_re, s5_b_im, s5_c_re, s5_c_im, s5_d, s5_w_glu, attn_w_qkv,
           attn_q_norm, attn_k_norm, attn_rpb, attn_w_o, ffn_w_gate, ffn_w_up, ffn_w_down,
           ple_w_gate, ple_w_proj):
    seq_chunks = tuple(n // CHUNK for n in seq_lens)
    for i in range(DEPTH):
        j = i // 2
        if i % 2 == 0:
            bd, cx, kt, tab = s5_prepare(s5_a_re[j], s5_a_im[j], s5_log_dt[j], s5_b_re[j],
                                         s5_b_im[j], s5_c_re[j], s5_c_im[j])
            u = s5_in_proj(h, norm_mix[i], s5_w_in[j].astype(BF16))
            g = s5_core(u, bd, cx, kt, tab, s5_d[j], seq_chunks=seq_chunks)
            h = s5_glu_residual(g, s5_w_glu[j].astype(BF16), h)
        else:
            qkv = qkv_proj(h, norm_mix[i], attn_w_qkv[j].astype(BF16), attn_q_norm[j],
                           attn_k_norm[j])
            att, row0 = None, 0
            for n in seq_lens:
                assert row0 % n == 0
                att = neighborhood_attention(qkv, attn_rpb[j], att, seq_len=n,
                                             row_block=row0 // n)
                row0 += n
            h = attn_out_residual(att, attn_w_o[j].astype(BF16), h)
        h = ffn_residual(h, norm_ffn[i], ffn_w_gate[i].astype(BF16), ffn_w_up[i].astype(BF16),
                         ffn_w_down[i].astype(BF16))
        h = ple_residual(h, norm_ple[i], ple_w_gate[i].astype(BF16), p_all[i],
                         ple_w_proj[i].astype(BF16),
                         split_rows=seq_lens[0] if i == DEPTH - 1 else None)
    return h


def kernel(x_prompt, x_sample, p_prompt, p_sample, norm_mix, norm_ffn, norm_ple, s5_w_in, s5_a_re, s5_a_im, s5_log_dt, s5_b_re, s5_b_im, s5_c_re, s5_c_im, s5_d, s5_w_glu, attn_w_qkv, attn_q_norm, attn_k_norm, attn_rpb, attn_w_o, ffn_w_gate, ffn_w_up, ffn_w_down, ple_w_gate, ple_w_proj):
    assert x_prompt.shape[0] == 1 and x_sample.shape[0] == 1
    n_s, n_p = x_sample.shape[1], x_prompt.shape[1]
    h = jnp.concatenate([x_sample[0], x_prompt[0]], axis=0)
    p_all = jnp.concatenate([p_sample[:, 0], p_prompt[:, 0]], axis=1).astype(BF16)
    y_sample, y_prompt = _trunk(
        h, p_all, (n_s, n_p), norm_mix, norm_ffn, norm_ple, s5_w_in, s5_a_re, s5_a_im,
        s5_log_dt, s5_b_re, s5_b_im, s5_c_re, s5_c_im, s5_d, s5_w_glu, attn_w_qkv,
        attn_q_norm, attn_k_norm, attn_rpb, attn_w_o, ffn_w_gate, ffn_w_up, ffn_w_down,
        ple_w_gate, ple_w_proj)
    return y_prompt[None], y_sample[None]
```

```python
import functools
import math

import numpy as np
import jax
import jax.numpy as jnp
from jax import lax
from jax.experimental import pallas as pl
from jax.experimental.pallas import tpu as pltpu

F32 = jnp.float32
BF16 = jnp.bfloat16

D_MODEL = 2048
DEPTH = 4
PLE_DIM = 256
GRID_W = 64
SSM_GROUP = 16
SSM_GROUPS = D_MODEL // SSM_GROUP
SSM_STATE = 64
HEAD_DIM = 128
N_HEADS = D_MODEL // HEAD_DIM
WIN_H = 8
WIN_W = 16
D_FF = 5632
EPS = 1e-6

LANES = 128
SUBLANES = 8
VMEM_LIMIT_BYTES = 56 * 1024 * 1024

TN = 512
CHUNK = 16
GROUPS_PER_TILE = LANES // SSM_GROUP
N_LANE_TILES = D_MODEL // LANES
STATE_LANES = GROUPS_PER_TILE * SSM_STATE
CW = CHUNK * LANES
N_LAGS = 2 * CHUNK - 1
PW_ROWS = 128
NEG_BIG = -1e30


def _params(sem, vmem=VMEM_LIMIT_BYTES):
    return pltpu.CompilerParams(dimension_semantics=sem, vmem_limit_bytes=vmem)


def _rmsnorm_bf16(x, gain):
    ms = jnp.mean(x * x, axis=-1, keepdims=True)
    return (x * lax.rsqrt(ms + EPS) * gain).astype(BF16)


def _dot(a, b):
    return jnp.dot(a, b, preferred_element_type=F32)


def _dot_split(a, b):
    a_hi, b_hi = a.astype(BF16), b.astype(BF16)
    a_lo = (a - a_hi.astype(F32)).astype(BF16)
    b_lo = (b - b_hi.astype(F32)).astype(BF16)
    return _dot(a_hi, b_hi) + (_dot(a_hi, b_lo) + _dot(a_lo, b_hi))


def _resident(shape):
    return pl.BlockSpec(shape, lambda *_: (0,) * len(shape), pipeline_mode=pl.Buffered(1))


def _col_chunks(width):
    return [slice(n * TN, (n + 1) * TN) for n in range(width // TN)]


def _in_proj_kernel(x_ref, g_ref, w_ref, o_ref, xn_ref):
    xn_ref[...] = _rmsnorm_bf16(x_ref[...], g_ref[...])
    for cols in _col_chunks(D_MODEL):
        o_ref[:, cols] = _dot(xn_ref[...], w_ref[:, cols])


def s5_in_proj(h, gain, w, *, tm=512):
    m = h.shape[0]
    tm = math.gcd(tm, m)
    return pl.pallas_call(
        _in_proj_kernel,
        grid=(m // tm,),
        in_specs=[
            pl.BlockSpec((tm, D_MODEL), lambda i: (i, 0)),
            _resident((1, D_MODEL)),
            _resident((D_MODEL, D_MODEL)),
        ],
        out_specs=pl.BlockSpec((tm, D_MODEL), lambda i: (i, 0)),
        out_shape=jax.ShapeDtypeStruct((m, D_MODEL), F32),
        scratch_shapes=[pltpu.VMEM((tm, D_MODEL), BF16)],
        compiler_params=_params(("parallel",)),
        name="s5_in_proj",
    )(h, gain.reshape(1, D_MODEL), w)


def _qkv_kernel(x_ref, g_ref, w_ref, qg_ref, kg_ref, o_ref, xn_ref):
    xn_ref[...] = _rmsnorm_bf16(x_ref[...], g_ref[...])
    q_gain = qg_ref[...] * HEAD_DIM ** -0.5
    k_gain = kg_ref[...]

    def head_norm(acc, gain):
        outs = []
        for hh in range(TN // HEAD_DIM):
            t = acc[:, hh * HEAD_DIM:(hh + 1) * HEAD_DIM]
            ms = jnp.mean(t * t, axis=-1, keepdims=True)
            outs.append(t * lax.rsqrt(ms + EPS) * gain)
        return jnp.concatenate(outs, axis=-1)

    for cols in _col_chunks(3 * D_MODEL):
        acc = _dot(xn_ref[...], w_ref[:, cols])
        if cols.start < D_MODEL:
            acc = head_norm(acc, q_gain)
        elif cols.start < 2 * D_MODEL:
            acc = head_norm(acc, k_gain)
        o_ref[:, cols] = acc.astype(o_ref.dtype)


def qkv_proj(h, gain, w, q_gain, k_gain, *, tm=512):
    m = h.shape[0]
    n_out = w.shape[1]
    tm = math.gcd(tm, m)
    return pl.pallas_call(
        _qkv_kernel,
        grid=(m // tm,),
        in_specs=[
            pl.BlockSpec((tm, D_MODEL), lambda i: (i, 0)),
            _resident((1, D_MODEL)),
            _resident((D_MODEL, n_out)),
            _resident((1, HEAD_DIM)),
            _resident((1, HEAD_DIM)),
        ],
        out_specs=pl.BlockSpec((tm, n_out), lambda i: (i, 0)),
        out_shape=jax.ShapeDtypeStruct((m, n_out), BF16),
        scratch_shapes=[pltpu.VMEM((tm, D_MODEL), BF16)],
        compiler_params=_params(("parallel",)),
        name="qkv_proj",
    )(h, gain.reshape(1, D_MODEL), w, q_gain.reshape(1, HEAD_DIM), k_gain.reshape(1, HEAD_DIM))


def _ffn_kernel(h_ref, g_ref, wg_ref, wu_ref, wd_ref, o_ref, xn_ref):
    @pl.when(pl.program_id(1) == 0)
    def _():
        x = h_ref[...]
        xn_ref[...] = _rmsnorm_bf16(x, g_ref[...])
        o_ref[...] = x

    xn = xn_ref[...]
    gate = _dot(xn, wg_ref[...])
    up = _dot(xn, wu_ref[...])
    act = (jax.nn.silu(gate) * up).astype(BF16)
    o_ref[...] += _dot(act, wd_ref[...])


def ffn_residual(h, gain, w_gate, w_up, w_down, *, tm=768, tf=512):
    m = h.shape[0]
    tm = math.gcd(tm, m)
    return pl.pallas_call(
        _ffn_kernel,
        grid=(m // tm, D_FF // tf),
        in_specs=[
            pl.BlockSpec((tm, D_MODEL), lambda i, f: (i, 0)),
            pl.BlockSpec((1, D_MODEL), lambda i, f: (0, 0)),
            pl.BlockSpec((D_MODEL, tf), lambda i, f: (0, f)),
            pl.BlockSpec((D_MODEL, tf), lambda i, f: (0, f)),
            pl.BlockSpec((tf, D_MODEL), lambda i, f: (f, 0)),
        ],
        out_specs=pl.BlockSpec((tm, D_MODEL), lambda i, f: (i, 0)),
        out_shape=jax.ShapeDtypeStruct((m, D_MODEL), F32),
        scratch_shapes=[pltpu.VMEM((tm, D_MODEL), BF16)],
        compiler_params=_params(("parallel", "arbitrary")),
        name="ffn_residual",
    )(h, gain.reshape(1, D_MODEL), w_gate, w_up, w_down)


def _ple_kernel(x_ref, g_ref, wg_ref, p_ref, wp_ref, *rest, first_tiles):
    o_refs, xn_ref = rest[:-1], rest[-1]
    xn_ref[...] = _rmsnorm_bf16(x_ref[...], g_ref[...])

    def emit(o_ref):
        for cols in _col_chunks(D_MODEL):
            z = _dot(xn_ref[...], wg_ref[:, cols])
            proj = _dot(p_ref[...], wp_ref[:, cols])
            o_ref[:, cols] = x_ref[:, cols] + jax.nn.sigmoid(z) * proj

    if len(o_refs) == 1:
        emit(o_refs[0])
    else:
        pl.when(pl.program_id(0) < first_tiles)(lambda: emit(o_refs[0]))
        pl.when(pl.program_id(0) >= first_tiles)(lambda: emit(o_refs[1]))


def ple_residual(h, gain, w_gate, p, w_proj, *, split_rows=None, tm=512):
    m = h.shape[0]
    tm = math.gcd(tm, m if split_rows is None else math.gcd(m, split_rows))
    if split_rows is None:
        first_tiles = m // tm
        out_specs = pl.BlockSpec((tm, D_MODEL), lambda i: (i, 0))
        out_shape = jax.ShapeDtypeStruct((m, D_MODEL), F32)
    else:
        first_tiles = split_rows // tm
        out_specs = [
            pl.BlockSpec((tm, D_MODEL), lambda i: (jnp.minimum(i, first_tiles - 1), 0)),
            pl.BlockSpec((tm, D_MODEL), lambda i: (jnp.maximum(i - first_tiles, 0), 0)),
        ]
        out_shape = [jax.ShapeDtypeStruct((split_rows, D_MODEL), F32),
                     jax.ShapeDtypeStruct((m - split_rows, D_MODEL), F32)]
    return pl.pallas_call(
        functools.partial(_ple_kernel, first_tiles=first_tiles),
        grid=(m // tm,),
        in_specs=[
            pl.BlockSpec((tm, D_MODEL), lambda i: (i, 0)),
            _resident((1, D_MODEL)),
            _resident((D_MODEL, D_MODEL)),
            pl.BlockSpec((tm, PLE_DIM), lambda i: (i, 0)),
            _resident((PLE_DIM, D_MODEL)),
        ],
        out_specs=out_specs,
        out_shape=out_shape,
        scratch_shapes=[pltpu.VMEM((tm, D_MODEL), BF16)],
        compiler_params=_params(("arbitrary",)),
        name="ple_residual",
    )(h, gain.reshape(1, D_MODEL), w_gate, p, w_proj)


def _glu_kernel(g_ref, w_ref, h_ref, o_ref):
    for n, cols in enumerate(_col_chunks(D_MODEL)):
        gate_cols = slice(D_MODEL + n * TN, D_MODEL + (n + 1) * TN)
        a = _dot(g_ref[...], w_ref[:, cols])
        b = _dot(g_ref[...], w_ref[:, gate_cols])
        o_ref[:, cols] = h_ref[:, cols] + a * jax.nn.sigmoid(b)


def s5_glu_residual(g, w_glu, h, *, tm=512):
    m = h.shape[0]
    tm = math.gcd(tm, m)
    return pl.pallas_call(
        _glu_kernel,
        grid=(m // tm,),
        in_specs=[
            pl.BlockSpec((tm, D_MODEL), lambda i: (i, 0)),
            _resident((D_MODEL, 2 * D_MODEL)),
            pl.BlockSpec((tm, D_MODEL), lambda i: (i, 0)),
        ],
        out_specs=pl.BlockSpec((tm, D_MODEL), lambda i: (i, 0)),
        out_shape=jax.ShapeDtypeStruct((m, D_MODEL), F32),
        compiler_params=_params(("parallel",)),
        name="s5_glu_residual",
    )(g, w_glu, h)


def _proj_residual_kernel(a_ref, w_ref, h_ref, o_ref):
    for cols in _col_chunks(D_MODEL):
        o_ref[:, cols] = h_ref[:, cols] + _dot(a_ref[...], w_ref[:, cols])


def attn_out_residual(a, w_o, h, *, tm=512):
    m = h.shape[0]
    tm = math.gcd(tm, m)
    return pl.pallas_call(
        _proj_residual_kernel,
        grid=(m // tm,),
        in_specs=[
            pl.BlockSpec((tm, D_MODEL), lambda i: (i, 0)),
            _resident((D_MODEL, D_MODEL)),
            pl.BlockSpec((tm, D_MODEL), lambda i: (i, 0)),
        ],
        out_specs=pl.BlockSpec((tm, D_MODEL), lambda i: (i, 0)),
        out_shape=jax.ShapeDtypeStruct((m, D_MODEL), F32),
        compiler_params=_params(("parallel",)),
        name="attn_out_residual",
    )(a, w_o, h)


def _attn_plan(rows, r_blk):
    span = r_blk + WIN_H - 1
    variants, vid = [], []
    for i in range(rows // r_blk):
        base = min(max(r_blk * i - WIN_H // 2, 0), rows - span)
        key = tuple(min(max(r - WIN_H // 2, 0), rows - WIN_H) - base
                    for r in range(r_blk * i, r_blk * (i + 1))) + (r_blk * i - base,)
        if key not in variants:
            variants.append(key)
        vid.append(variants.index(key))
    return span, variants, np.asarray(vid, np.int32)


def _attn_bias_tables(rpb, rows, r_blk):
    span, variants, vid = _attn_plan(rows, r_blk)
    cols = np.arange(GRID_W)
    col_start = np.clip(cols - WIN_W // 2, 0, GRID_W - WIN_W)
    in_win = (cols[None, :] >= col_start[:, None]) & (cols[None, :] < col_start[:, None] + WIN_W)
    pad = GRID_W - WIN_W
    rp = jnp.pad(rpb.astype(F32), ((0, 0), (0, 0), (pad, pad)))
    band = jnp.stack([rp[:, :, GRID_W - 1 - c:2 * GRID_W - 1 - c] for c in range(GRID_W)], axis=2)
    band = jnp.where(in_win[None, None], band, NEG_BIG)
    wins = [jnp.transpose(band[:, d0:d0 + WIN_H], (0, 2, 1, 3)).reshape(
        N_HEADS, GRID_W, WIN_H * GRID_W) for d0 in range(WIN_H)]
    tables = []
    for key in variants:
        q0 = key[-1]
        row_blocks = []
        for rl in range(r_blk):
            rs = key[rl]
            d0 = rs - (q0 + rl) + WIN_H - 1
            row_blocks.append(jnp.pad(
                wins[d0], ((0, 0), (0, 0), (rs * GRID_W, (span - rs - WIN_H) * GRID_W)),
                constant_values=NEG_BIG))
        tables.append(jnp.concatenate(row_blocks, axis=1))
    return jnp.stack(tables, axis=1), vid


def _attn_kernel(vid_ref, q_ref, k_ref, v_ref, b_ref, *rest, rows, r_blk, unroll):
    o_ref = rest[-1]
    span = r_blk + WIN_H - 1
    nq, nk = r_blk * GRID_W, span * GRID_W

    def body(i, carry):
        base = jnp.clip(r_blk * i - WIN_H // 2, 0, rows - span)
        q0 = pl.multiple_of(i * nq, GRID_W)
        k0 = pl.multiple_of(base * GRID_W, GRID_W)
        q = q_ref[pl.ds(q0, nq), :]
        kb = k_ref[pl.ds(k0, nk), :]
        vb = v_ref[pl.ds(k0, nk), :]
        s = lax.dot_general(q, kb, (((1,), (1,)), ((), ())), preferred_element_type=F32)
        s = s + b_ref[vid_ref[i]]
        m = jnp.max(s, axis=-1, keepdims=True)
        e = jnp.exp(s - m)
        denom = jnp.sum(e, axis=-1, keepdims=True)
        o = _dot(e.astype(BF16), vb) / denom
        o_ref[pl.ds(q0, nq), :] = o.astype(o_ref.dtype)
        return carry

    lax.fori_loop(0, rows // r_blk, body, 0, unroll=unroll)


def neighborhood_attention(qkv, rpb, prev_out, *, seq_len, row_block, r_blk=8, unroll=4):
    rows = seq_len // GRID_W
    bias, vid = _attn_bias_tables(rpb, rows, r_blk)
    n_var, nq, nk = bias.shape[1:]
    n_blk = rows // r_blk
    kern = functools.partial(_attn_kernel, rows=rows, r_blk=r_blk,
                             unroll=math.gcd(unroll, n_blk))
    in_specs = [
        pl.BlockSpec((seq_len, HEAD_DIM), lambda h, vid: (row_block, h)),
        pl.BlockSpec((seq_len, HEAD_DIM), lambda h, vid: (row_block, N_HEADS + h)),
        pl.BlockSpec((seq_len, HEAD_DIM), lambda h, vid: (row_block, 2 * N_HEADS + h)),
        pl.BlockSpec((None, n_var, nq, nk), lambda h, vid: (h, 0, 0, 0)),
    ]
    args = [jnp.asarray(vid), qkv, qkv, qkv, bias]
    aliases = {}
    if prev_out is not None:
        in_specs.append(pl.BlockSpec(memory_space=pl.ANY))
        args.append(prev_out)
        aliases = {len(args) - 1: 0}
    grid_spec = pltpu.PrefetchScalarGridSpec(
        num_scalar_prefetch=1,
        grid=(N_HEADS,),
        in_specs=in_specs,
        out_specs=pl.BlockSpec((seq_len, HEAD_DIM), lambda h, vid: (row_block, h)),
    )
    return pl.pallas_call(
        kern,
        grid_spec=grid_spec,
        out_shape=jax.ShapeDtypeStruct((qkv.shape[0], D_MODEL), BF16),
        input_output_aliases=aliases,
        compiler_params=_params(("arbitrary",)),
        name="neighborhood_attention",
    )(*args)


def _pw_exponents():
    e = np.zeros((PW_ROWS, 1), np.float32)
    for r in range(CHUNK + 1):
        e[r, 0] = r
    for r in range(CHUNK + 1, CHUNK + SUBLANES):
        e[r, 0] = CHUNK * (r - CHUNK + 1)
    for r in range(CHUNK + SUBLANES, CHUNK + 2 * SUBLANES):
        e[r, 0] = CHUNK * (CHUNK + 2 * SUBLANES - r)
    return e


def _s5_prep_kernel(e_ref, prow_ref, b_ref, c_ref, bd_ref, cx_ref, kt_ref, tab_ref, blk_ref):
    expo = e_ref[...]
    row_g = lax.broadcasted_iota(jnp.int32, (LANES, STATE_LANES), 0) // SSM_GROUP
    col_g = lax.broadcasted_iota(jnp.int32, (LANES, STATE_LANES), 1) // SSM_STATE
    b_mask = row_g == col_g
    row_gc = lax.broadcasted_iota(jnp.int32, (STATE_LANES, LANES), 0) // SSM_STATE
    col_gc = lax.broadcasted_iota(jnp.int32, (STATE_LANES, LANES), 1) // SSM_GROUP
    c_mask = row_gc == col_gc
    sub = lax.broadcasted_iota(jnp.int32, (SUBLANES, STATE_LANES), 0)

    k_same = None
    for d in range(2):
        a_re = prow_ref[d, 0]
        a_im = prow_ref[d, 1]
        dt = jnp.exp(prow_ref[d, 2])
        mag = jnp.exp(expo * (a_re * dt))
        ang = expo * (a_im * dt)
        pw_re = mag * jnp.cos(ang)
        pw_im = mag * jnp.sin(ang)
        pw_re_t = pw_re.T
        pw_im_t = pw_im.T

        xr = pw_re[1:2] - 1.0
        xi = pw_im[1:2]
        den = a_re * a_re + a_im * a_im
        f_re = (xr * a_re + xi * a_im) / den
        f_im = (xi * a_re - xr * a_im) / den
        b_re = jnp.where(b_mask, b_ref[d, 0], 0.0)
        b_im = jnp.where(b_mask, b_ref[d, 1], 0.0)
        bb_re = f_re * b_re - f_im * b_im
        bb_im = f_re * b_im + f_im * b_re
        c_re = jnp.where(c_mask, c_ref[d, 0], 0.0)
        c_im = jnp.where(c_mask, c_ref[d, 1], 0.0)
        c_cat = jnp.concatenate([c_re, -c_im], axis=0)

        col0 = 2 * d * STATE_LANES
        for s in range(CHUNK):
            n_in = CHUNK - 1 - s if d == 0 else s
            pr, pi = pw_re[n_in:n_in + 1], pw_im[n_in:n_in + 1]
            blk_re = bb_re * pr - bb_im * pi
            blk_im = bb_re * pi + bb_im * pr
            rows = slice(s * LANES, (s + 1) * LANES)
            bd_ref[rows, col0:col0 + STATE_LANES] = blk_re.astype(bd_ref.dtype)
            bd_ref[rows, col0 + STATE_LANES:col0 + 2 * STATE_LANES] = blk_im.astype(bd_ref.dtype)
            blk_ref[rows, 0:STATE_LANES] = blk_re
            blk_ref[rows, STATE_LANES:2 * STATE_LANES] = blk_im

            n_out = s + 1 if d == 0 else CHUNK - s
            qr = pw_re_t[:, n_out:n_out + 1]
            qi = pw_im_t[:, n_out:n_out + 1]
            cols = slice(s * LANES, (s + 1) * LANES)
            cx_ref[col0:col0 + STATE_LANES, cols] = (c_re * qr - c_im * qi).astype(cx_ref.dtype)
            cx_ref[col0 + STATE_LANES:col0 + 2 * STATE_LANES, cols] = (
                -(c_re * qi + c_im * qr)).astype(cx_ref.dtype)

        k_all = _dot_split(blk_ref[...], c_cat)
        for s in range(CHUNK):
            n_in = CHUNK - 1 - s if d == 0 else s
            k_lag = k_all[s * LANES:(s + 1) * LANES]
            if n_in == 0:
                k_same = k_lag if k_same is None else k_same + k_lag
            else:
                lag = n_in if d == 0 else -n_in
                kt_ref[CHUNK - 1 + lag] = k_lag.astype(kt_ref.dtype)

        for k, sh in enumerate((1, 2, 4)):
            r = CHUNK + sh - 1 if sh < 4 else CHUNK + 3
            keep = (sub >= sh) if d == 0 else (sub < SUBLANES - sh)
            tab_ref[d, 2 * k] = jnp.where(keep, pw_re[r:r + 1], 0.0)
            tab_ref[d, 2 * k + 1] = jnp.where(keep, pw_im[r:r + 1], 0.0)
        p0 = CHUNK if d == 0 else CHUNK + SUBLANES
        tab_ref[d, 6] = pw_re[p0:p0 + SUBLANES]
        tab_ref[d, 7] = pw_im[p0:p0 + SUBLANES]
    kt_ref[CHUNK - 1] = k_same.astype(kt_ref.dtype)


def _tile_params(x):
    x = x.reshape(2, N_LANE_TILES, 1, STATE_LANES)
    return jnp.transpose(x, (1, 0, 2, 3))


def s5_prepare(a_re, a_im, log_dt, b_re, b_im, c_re, c_im):
    g, p, gc = SSM_GROUPS, SSM_STATE, SSM_GROUP
    prow = jnp.stack([_tile_params(a_re), _tile_params(a_im),
                      _tile_params(jnp.broadcast_to(log_dt[:, :, None], (2, g, p)))], axis=2)

    def b_tiles(b):
        b = b.reshape(2, N_LANE_TILES, GROUPS_PER_TILE, p, gc)
        b = jnp.transpose(b, (1, 0, 4, 2, 3)).reshape(N_LANE_TILES, 2, 1, gc, STATE_LANES)
        return jnp.broadcast_to(b, (N_LANE_TILES, 2, GROUPS_PER_TILE, gc, STATE_LANES)).reshape(
            N_LANE_TILES, 2, LANES, STATE_LANES)

    def c_tiles(c):
        c = c.reshape(2, N_LANE_TILES, GROUPS_PER_TILE, gc, p)
        c = jnp.transpose(c, (1, 0, 2, 4, 3)).reshape(N_LANE_TILES, 2, STATE_LANES, 1, gc)
        return jnp.broadcast_to(c, (N_LANE_TILES, 2, STATE_LANES, GROUPS_PER_TILE, gc)).reshape(
            N_LANE_TILES, 2, STATE_LANES, LANES)

    b_t = jnp.stack([b_tiles(b_re), b_tiles(b_im)], axis=2)
    c_t = jnp.stack([c_tiles(c_re), c_tiles(c_im)], axis=2)
    expo = jnp.asarray(_pw_exponents())

    return pl.pallas_call(
        _s5_prep_kernel,
        grid=(N_LANE_TILES,),
        in_specs=[
            pl.BlockSpec((PW_ROWS, 1), lambda j: (0, 0)),
            pl.BlockSpec((None, 2, 3, 1, STATE_LANES), lambda j: (j, 0, 0, 0, 0)),
            pl.BlockSpec((None, 2, 2, LANES, STATE_LANES), lambda j: (j, 0, 0, 0, 0)),
            pl.BlockSpec((None, 2, 2, STATE_LANES, LANES), lambda j: (j, 0, 0, 0, 0)),
        ],
        out_specs=[
            pl.BlockSpec((None, CW, 4 * STATE_LANES), lambda j: (j, 0, 0)),
            pl.BlockSpec((None, 4 * STATE_LANES, CW), lambda j: (j, 0, 0)),
            pl.BlockSpec((None, N_LAGS, LANES, LANES), lambda j: (j, 0, 0, 0)),
            pl.BlockSpec((None, 2, 8, SUBLANES, STATE_LANES), lambda j: (j, 0, 0, 0, 0)),
        ],
        out_shape=[
            jax.ShapeDtypeStruct((N_LANE_TILES, CW, 4 * STATE_LANES), BF16),
            jax.ShapeDtypeStruct((N_LANE_TILES, 4 * STATE_LANES, CW), BF16),
            jax.ShapeDtypeStruct((N_LANE_TILES, N_LAGS, LANES, LANES), BF16),
            jax.ShapeDtypeStruct((N_LANE_TILES, 2, 8, SUBLANES, STATE_LANES), F32),
        ],
        scratch_shapes=[pltpu.VMEM((CW, 2 * STATE_LANES), F32)],
        compiler_params=_params(("arbitrary",)),
        name="s5_prepare",
    )(expo, prow, b_t, c_t)


def _cmul_add(xr, xi, ar, ai, yr, yi):
    return xr + ar * yr - ai * yi, xi + ar * yi + ai * yr


def _scan_sequence(x_ref, tab_ref, row0, n_rows):
    sl = STATE_LANES
    n_blk = n_rows // SUBLANES
    sub = lax.broadcasted_iota(jnp.int32, (SUBLANES, sl), 0)

    def one_block(d, row, carry):
        c0 = 2 * d * sl
        xr = x_ref[pl.ds(row, SUBLANES), c0:c0 + sl]
        xi = x_ref[pl.ds(row, SUBLANES), c0 + sl:c0 + 2 * sl]
        for k, sh in enumerate((1, 2, 4)):
            shift = sh if d == 0 else SUBLANES - sh
            yr = pltpu.roll(xr, shift, 0)
            yi = pltpu.roll(xi, shift, 0)
            xr, xi = _cmul_add(xr, xi, tab_ref[d, 2 * k], tab_ref[d, 2 * k + 1], yr, yi)
        cr = jnp.broadcast_to(carry[0], (SUBLANES, sl))
        ci = jnp.broadcast_to(carry[1], (SUBLANES, sl))
        xr, xi = _cmul_add(xr, xi, tab_ref[d, 6], tab_ref[d, 7], cr, ci)
        shift, vacated = (1, 0) if d == 0 else (SUBLANES - 1, SUBLANES - 1)
        x_ref[pl.ds(row, SUBLANES), c0:c0 + sl] = jnp.where(
            sub == vacated, cr, pltpu.roll(xr, shift, 0))
        x_ref[pl.ds(row, SUBLANES), c0 + sl:c0 + 2 * sl] = jnp.where(
            sub == vacated, ci, pltpu.roll(xi, shift, 0))
        edge = SUBLANES - 1 if d == 0 else 0
        return xr[edge:edge + 1], xi[edge:edge + 1]

    def body(i, carry):
        f_row = pl.multiple_of(row0 + i * SUBLANES, SUBLANES)
        b_row = pl.multiple_of(row0 + (n_blk - 1 - i) * SUBLANES, SUBLANES)
        cf = one_block(0, f_row, carry[0:2])
        cb = one_block(1, b_row, carry[2:4])
        return cf + cb

    z = jnp.zeros((1, sl), F32)
    lax.fori_loop(0, n_blk, body, (z, z, z, z))


def _s5_core_kernel(u_ref, bd_ref, cx_ref, kt_ref, tab_ref, d_ref, o_ref,
                    lhs_ref, x_ref, m_ref, stage_ref, *, bk, seq_chunks):
    phase = pl.program_id(1)
    cb = pl.program_id(2)
    n_cb = pl.num_programs(2)
    off = pl.multiple_of(cb * bk, bk)
    starts = [sum(seq_chunks[:k]) for k in range(len(seq_chunks))]

    u_slots = [u_ref[pl.ds(s, bk, stride=CHUNK), :] for s in range(CHUNK)]
    for s in range(CHUNK):
        lhs_ref[:, s * LANES:(s + 1) * LANES] = u_slots[s].astype(BF16)

    @pl.when(phase == 0)
    def _():
        x_ref[pl.ds(off, bk), :] = _dot(lhs_ref[...], bd_ref[...])

        @pl.when(cb == n_cb - 1)
        def _():
            for st, n in zip(starts, seq_chunks):
                _scan_sequence(x_ref, tab_ref, st, n)

    @pl.when(phase == 1)
    def _():
        @pl.when(cb == 0)
        def _():
            for s in range(CHUNK):
                for t in range(CHUNK):
                    m_ref[s * LANES:(s + 1) * LANES, t * LANES:(t + 1) * LANES] = (
                        kt_ref[CHUNK - 1 + t - s])

        y = _dot(lhs_ref[...], m_ref[...])
        y = y + _dot(x_ref[pl.ds(off, bk), :].astype(BF16), cx_ref[...])
        dvec = d_ref[...]
        for t in range(CHUNK):
            yt = y[:, t * LANES:(t + 1) * LANES] + dvec * u_slots[t]
            stage_ref[pl.ds(t, bk, stride=CHUNK), :] = jax.nn.gelu(yt)
        o_ref[...] = stage_ref[...].astype(o_ref.dtype)


def s5_core(u, bd, cx, kt, tab, d, *, seq_chunks, bk=128):
    m = u.shape[0]
    bk = math.gcd(bk, *seq_chunks)
    n_chunks = m // CHUNK
    assert n_chunks == sum(seq_chunks)
    kern = functools.partial(_s5_core_kernel, bk=bk, seq_chunks=tuple(seq_chunks))
    single = pl.Buffered(1)
    return pl.pallas_call(
        kern,
        grid=(N_LANE_TILES, 2, n_chunks // bk),
        in_specs=[
            pl.BlockSpec((bk * CHUNK, LANES), lambda j, ph, c: (c, j)),
            pl.BlockSpec((None, CW, 4 * STATE_LANES), lambda j, ph, c: (j, 0, 0),
                         pipeline_mode=single),
            pl.BlockSpec((None, 4 * STATE_LANES, CW), lambda j, ph, c: (j, 0, 0),
                         pipeline_mode=single),
            pl.BlockSpec((None, N_LAGS, LANES, LANES), lambda j, ph, c: (j, 0, 0, 0)),
            pl.BlockSpec((None, 2, 8, SUBLANES, STATE_LANES), lambda j, ph, c: (j, 0, 0, 0, 0)),
            pl.BlockSpec((1, LANES), lambda j, ph, c: (0, j)),
        ],
        out_specs=pl.BlockSpec((bk * CHUNK, LANES), lambda j, ph, c: (c * ph, j)),
        out_shape=jax.ShapeDtypeStruct((m, D_MODEL), BF16),
        scratch_shapes=[
            pltpu.VMEM((bk, CW), BF16),
            pltpu.VMEM((n_chunks, 4 * STATE_LANES), F32),
            pltpu.VMEM((CW, CW), BF16),
            pltpu.VMEM((bk * CHUNK, LANES), F32),
        ],
        compiler_params=_params(("arbitrary", "arbitrary", "arbitrary")),
        name="s5_core",
    )(u, bd, cx, kt, tab, d.reshape(1, D_MODEL))


def _trunk(h, p_all, seq_lens, norm_mix, norm_ffn, norm_ple, s5_w_in, s5_a_re, s5_a_im,
           s5_log_dt, s5_b_re, s5_b_im, s5_c_re, s5_c_im, s5_d, s5_w_glu, attn_w_qkv,
           attn_q_norm, attn_k_norm, attn_rpb, attn_w_o, ffn_w_gate, ffn_w_up, ffn_w_down,
           ple_w_gate, ple_w_proj):
    seq_chunks = tuple(n // CHUNK for n in seq_lens)
    for i in range(DEPTH):
        j = i // 2
        if i % 2 == 0:
            bd, cx, kt, tab = s5_prepare(s5_a_re[j], s5_a_im[j], s5_log_dt[j], s5_b_re[j],
                                         s5_b_im[j], s5_c_re[j], s5_c_im[j])
            u = s5_in_proj(h, norm_mix[i], s5_w_in[j].astype(BF16))
            g = s5_core(u, bd, cx, kt, tab, s5_d[j], seq_chunks=seq_chunks)
            h = s5_glu_residual(g, s5_w_glu[j].astype(BF16), h)
        else:
            qkv = qkv_proj(h, norm_mix[i], attn_w_qkv[j].astype(BF16), attn_q_norm[j],
                           attn_k_norm[j])
            att, row0 = None, 0
            for n in seq_lens:
                assert row0 % n == 0
                att = neighborhood_attention(qkv, attn_rpb[j], att, seq_len=n,
                                             row_block=row0 // n)
                row0 += n
            h = attn_out_residual(att, attn_w_o[j].astype(BF16), h)
        h = ffn_residual(h, norm_ffn[i], ffn_w_gate[i].astype(BF16), ffn_w_up[i].astype(BF16),
                         ffn_w_down[i].astype(BF16))
        h = ple_residual(h, norm_ple[i], ple_w_gate[i].astype(BF16), p_all[i],
                         ple_w_proj[i].astype(BF16),
                         split_rows=seq_lens[0] if i == DEPTH - 1 else None)
    return h


def kernel(x_prompt, x_sample, p_prompt, p_sample, norm_mix, norm_ffn, norm_ple, s5_w_in, s5_a_re, s5_a_im, s5_log_dt, s5_b_re, s5_b_im, s5_c_re, s5_c_im, s5_d, s5_w_glu, attn_w_qkv, attn_q_norm, attn_k_norm, attn_rpb, attn_w_o, ffn_w_gate, ffn_w_up, ffn_w_down, ple_w_gate, ple_w_proj):
    assert x_prompt.shape[0] == 1 and x_sample.shape[0] == 1
    n_s, n_p = x_sample.shape[1], x_prompt.shape[1]
    h = jnp.concatenate([x_sample[0], x_prompt[0]], axis=0)
    p_all = jnp.concatenate([p_sample[:, 0], p_prompt[:, 0]], axis=1).astype(BF16)
    y_sample, y_prompt = _trunk(
        h, p_all, (n_s, n_p), norm_mix, norm_ffn, norm_ple, s5_w_in, s5_a_re, s5_a_im,
        s5_log_dt, s5_b_re, s5_b_im, s5_c_re, s5_c_im, s5_d, s5_w_glu, attn_w_qkv,
        attn_q_norm, attn_k_norm, attn_rpb, attn_w_o, ffn_w_gate, ffn_w_up, ffn_w_down,
        ple_w_gate, ple_w_proj)
    return y_prompt[None], y_sample[None]
```
